```python
import math
import jax, jax.numpy as jnp
from jax import lax
import numpy as np

D_MODEL = 4096
BATCH = 1
SEQ = 16384
DEPTH = 2
DEC_BATCH = 32
DEC_SEQ = 32
PAST_LEN = 1024

CHUNK = 64
EPS = 1e-6
MIX_W = D_MODEL // 2
D_FF = 256 * ((8 * D_MODEL // 3 + 255) // 256)
N_BRANCH = 4
A_NOPE = 128
A_ROPE = 64
A_VDIM = 128
A_HEADS = MIX_W // A_VDIM
Q_LORA = D_MODEL // 4
KV_LORA = 512
ROPE_THETA = 10000.0
A_QBLOCK = 128
B_WIDTH = MIX_W
B_BLOCKS = 16
B_BW = B_WIDTH // B_BLOCKS
CONV_W = 4
LRU_C = 8.0
C_HDIM = 128
C_HEADS = MIX_W // C_HDIM
BAND_CHUNKS = 8
REL_CLIP = 256
D_WIDTH = MIX_W
S5_GC = 16
S5_GROUPS = D_WIDTH // S5_GC
S5_N = 64
A_COLS = Q_LORA + KV_LORA + A_ROPE
C_COLS = 3 * C_HEADS * C_HDIM
IN_SPLITS = (Q_LORA, Q_LORA + KV_LORA, A_COLS, A_COLS + B_WIDTH, A_COLS + B_WIDTH + C_COLS,
             A_COLS + B_WIDTH + C_COLS + D_WIDTH)
IN_WIDTH = A_COLS + B_WIDTH + C_COLS + D_WIDTH + N_BRANCH * D_MODEL

kernel_name = 'hybrid_streaming_encoder_step'


def rms_norm(x, g):
    x32 = x.astype(jnp.float32)
    y = x32 * lax.rsqrt(jnp.mean(x32 * x32, axis=-1, keepdims=True) + EPS)
    return (y * g.astype(jnp.float32)).astype(x.dtype)


def swiglu(x, w_up, w_down):
    gate, up = jnp.split(x @ w_up, 2, axis=-1)
    return (jax.nn.silu(gate) * up) @ w_down


def rope(x, pos):
    half = x.shape[-1] // 2
    inv = ROPE_THETA ** (-jnp.arange(half, dtype=jnp.float32) / half)
    ang = pos.astype(jnp.float32)[:, None] * inv
    ang = ang.reshape(ang.shape[:1] + (1,) * (x.ndim - 3) + (half,))
    cos, sin = jnp.cos(ang), jnp.sin(ang)
    x32 = x.astype(jnp.float32)
    x1, x2 = x32[..., :half], x32[..., half:]
    return jnp.concatenate([x1 * cos - x2 * sin, x2 * cos + x1 * sin], axis=-1).astype(x.dtype)


def linear_scan(a, b):
    def combine(l, r):
        return l[0] * r[0], r[0] * l[1] + r[1]
    return lax.associative_scan(combine, (a, b), axis=1)


def mla_attend(q_nope, q_rope, q_pos, k_nope, k_rope, v, k_pos):
    s = jnp.einsum('bqhd,bkhd->bhqk', q_nope, k_nope) + jnp.einsum('bqhr,bkr->bhqk', q_rope, k_rope)
    s = s.astype(jnp.float32) * (A_NOPE + A_ROPE) ** -0.5
    visible = (k_pos[None, :] // CHUNK) <= (q_pos[:, None] // CHUNK)
    s = jnp.where(visible, s, -jnp.inf)
    p = jax.nn.softmax(s, axis=-1).astype(v.dtype)
    return jnp.einsum('bhqk,bkhd->bqhd', p, v)


def mla_mixer(c_q, c_kv, k_r, pos, ckv_cache, kr_cache, q_norm, kv_norm, w_uq, w_ukv):
    bsz, seq, _ = c_q.shape
    q = (rms_norm(c_q, q_norm) @ w_uq).reshape(bsz, seq, A_HEADS, A_NOPE + A_ROPE)
    q_nope, q_rope = q[..., :A_NOPE], rope(q[..., A_NOPE:], pos)
    ckv_new = rms_norm(c_kv, kv_norm)
    kr_new = rope(k_r, pos)
    if ckv_cache is None:
        ckv_all, kr_all, k_pos = ckv_new, kr_new, pos
    else:
        ckv_all = jnp.concatenate([ckv_cache.astype(ckv_new.dtype), ckv_new], axis=1)
        kr_all = jnp.concatenate([kr_cache.astype(kr_new.dtype), kr_new], axis=1)
        k_pos = jnp.concatenate([jnp.arange(ckv_cache.shape[1], dtype=jnp.int32), pos])
    kv = (ckv_all @ w_ukv).reshape(bsz, -1, A_HEADS, A_NOPE + A_VDIM)
    k_nope, v = kv[..., :A_NOPE], kv[..., A_NOPE:]
    if ckv_cache is None:
        nb = seq // A_QBLOCK

        def split(t):
            return jnp.moveaxis(t.reshape((bsz, nb, A_QBLOCK) + t.shape[2:]), 1, 0)

        def one_block(args):
            qn, qr, qp = args
            return mla_attend(qn, qr, qp, k_nope, kr_all, v, k_pos)

        o = lax.map(one_block, (split(q_nope), split(q_rope), pos.reshape(nb, A_QBLOCK)))
        o = jnp.moveaxis(o, 0, 1).reshape(bsz, seq, A_HEADS * A_VDIM)
    else:
        o = mla_attend(q_nope, q_rope, pos, k_nope, kr_all, v, k_pos).reshape(bsz, seq, A_HEADS * A_VDIM)
    return o, ckv_new, kr_new


def causal_conv(x, buf, w, b):
    seq = x.shape[1]
    xp = jnp.concatenate([buf.astype(x.dtype), x], axis=1)
    y = b + sum(xp[:, k:k + seq] * w[k] for k in range(CONV_W))
    return y, xp[:, -(CONV_W - 1):]


def rglru_mixer(u, conv_buf, h0, conv_w, conv_b, w_a, b_a, w_x, b_x, lam):
    bsz, seq, wid = u.shape
    xc, new_buf = causal_conv(u, conv_buf, conv_w, conv_b)
    xb = xc.reshape(bsz, seq, B_BLOCKS, B_BW)
    r = jax.nn.sigmoid(jnp.einsum('bsnc,ncd->bsnd', xb, w_a).reshape(bsz, seq, wid) + b_a)
    i = jax.nn.sigmoid(jnp.einsum('bsnc,ncd->bsnd', xb, w_x).reshape(bsz, seq, wid) + b_x)
    log_a = -LRU_C * jax.nn.softplus(-lam.astype(jnp.float32)) * r.astype(jnp.float32)
    a = jnp.exp(log_a)
    b = jnp.sqrt(-jnp.expm1(2.0 * log_a)) * (i * xc).astype(jnp.float32)
    a_cum, h = linear_scan(a, b)
    h = h + a_cum * h0.astype(jnp.float32)[:, None]
    return h.astype(u.dtype), h[:, -1].astype(u.dtype), new_buf


def band_attend(q, q_pos, k, v, k_pos, rel_bias):
    s = jnp.einsum('bqhd,bkhd->bhqk', q, k).astype(jnp.float32) * C_HDIM ** -0.5
    rel = jnp.clip(q_pos[:, None] - k_pos[None, :], -REL_CLIP, REL_CLIP) + REL_CLIP
    s = s + jnp.take(rel_bias, rel, axis=1).astype(jnp.float32)
    qc, kc = q_pos[:, None] // CHUNK, k_pos[None, :] // CHUNK
    visible = (k_pos[None, :] >= 0) & (kc <= qc) & (kc >= qc - BAND_CHUNKS)
    s = jnp.where(visible, s, -jnp.inf)
    p = jax.nn.softmax(s, axis=-1).astype(v.dtype)
    return jnp.einsum('bhqk,bkhd->bqhd', p, v)


def chunk_band_mixer(q, k, v, pos, k_cache, v_cache, rel_bias):
    bsz, seq = q.shape[:2]
    if k_cache is None:
        pad = BAND_CHUNKS * CHUNK
        band = pad + CHUNK
        kp = jnp.pad(k, ((0, 0), (pad, 0), (0, 0), (0, 0)))
        vp = jnp.pad(v, ((0, 0), (pad, 0), (0, 0), (0, 0)))
        pos_p = jnp.arange(-pad, seq, dtype=jnp.int32)

        def one_chunk(c):
            start = c * CHUNK
            return band_attend(lax.dynamic_slice_in_dim(q, start, CHUNK, axis=1),
                               lax.dynamic_slice_in_dim(pos, start, CHUNK),
                               lax.dynamic_slice_in_dim(kp, start, band, axis=1),
                               lax.dynamic_slice_in_dim(vp, start, band, axis=1),
                               lax.dynamic_slice_in_dim(pos_p, start, band), rel_bias)

        o = lax.map(one_chunk, jnp.arange(seq // CHUNK, dtype=jnp.int32))
        o = jnp.moveaxis(o, 0, 1).reshape(bsz, seq, C_HEADS, C_HDIM)
        keep = min(pad, seq)
        return o, k[:, seq - keep:], v[:, seq - keep:]
    win = k_cache.shape[1]
    k_all = jnp.concatenate([k_cache.astype(k.dtype), k], axis=1)
    v_all = jnp.concatenate([v_cache.astype(v.dtype), v], axis=1)
    k_pos = jnp.concatenate([pos[0] - win + jnp.arange(win, dtype=jnp.int32), pos])
    return band_attend(q, pos, k_all, v_all, k_pos, rel_bias), k, v


def s5_mixer(u, x0_re, x0_im, a_re, a_im, log_dt, b_re, b_im, c_re, c_im, d_skip, w_glu, b_glu):
    bsz, seq, wid = u.shape
    f32 = jnp.float32
    u32 = u.astype(f32).reshape(bsz, seq, S5_GROUPS, S5_GC)
    lam = lax.complex(a_re.astype(f32), a_im.astype(f32))
    lam_dt = lam * jnp.exp(log_dt.astype(f32))[:, None]
    b_bar = ((jnp.exp(lam_dt) - 1.0) / lam)[..., None] * lax.complex(b_re.astype(f32), b_im.astype(f32))
    c_mat = lax.complex(c_re.astype(f32), c_im.astype(f32))
    bu = jnp.einsum('bsgc,gnc->bsgn', u32.astype(jnp.complex64), b_bar)
    steps = jnp.ones((1, seq, 1, 1), f32)

    def combine(l, r):
        return l[0] + r[0], jnp.exp(lam_dt * r[0]) * l[1] + r[1]

    n, xs = lax.associative_scan(combine, (steps, bu), axis=1)
    x0 = lax.complex(x0_re.astype(f32), x0_im.astype(f32))
    xs = xs + jnp.exp(lam_dt * n) * x0[:, None]
    y = jnp.real(jnp.einsum('bsgn,gcn->bsgc', xs, c_mat)) + d_skip.astype(f32).reshape(S5_GROUPS, S5_GC) * u32
    z = jax.nn.gelu(y.reshape(bsz, seq, wid))
    out = z * jax.nn.sigmoid(z @ w_glu.astype(f32) + b_glu.astype(f32))
    last = xs[:, -1]
    return out.astype(u.dtype), jnp.real(last).astype(u.dtype), jnp.imag(last).astype(u.dtype)


def trunk_layer(x, pos, ckv_c, kr_c, h_c, conv_c, bk_c, bv_c, s5re_c, s5im_c,
                norm_g, ffn_w_up, ffn_w_down, w_in, mla_q_norm, mla_kv_norm, mla_w_uq, mla_w_ukv,
                lru_conv_w, lru_conv_b, lru_w_a, lru_b_a, lru_w_x, lru_b_x, lru_lambda,
                band_rel_bias, s5_a_re, s5_a_im, s5_log_dt, s5_b_re, s5_b_im, s5_c_re, s5_c_im,
                s5_d, s5_w_glu, s5_b_glu, w_branch, w_out):
    bsz, seq, _ = x.shape
    x = x + 0.5 * rms_norm(swiglu(rms_norm(x, norm_g[0]), ffn_w_up[0], ffn_w_down[0]), norm_g[1])
    cols = rms_norm(x, norm_g[2]) @ w_in
    c_q, c_kv, k_r, u_b, qkv_c, u_d, gates = jnp.split(cols, IN_SPLITS, axis=-1)
    if h_c is None:
        h_c = jnp.zeros((bsz, B_WIDTH), x.dtype)
        conv_c = jnp.zeros((bsz, CONV_W - 1, B_WIDTH), x.dtype)
        s5re_c = jnp.zeros((bsz, S5_GROUPS, S5_N), x.dtype)
        s5im_c = s5re_c
    o_a, ckv_new, kr_new = mla_mixer(c_q, c_kv, k_r, pos, ckv_c, kr_c,
                                     mla_q_norm, mla_kv_norm, mla_w_uq, mla_w_ukv)
    o_b, h_new, conv_new = rglru_mixer(u_b, conv_c, h_c, lru_conv_w, lru_conv_b,
                                       lru_w_a, lru_b_a, lru_w_x, lru_b_x, lru_lambda)
    q_c, k_c, v_c = (t.reshape(bsz, seq, C_HEADS, C_HDIM) for t in jnp.split(qkv_c, 3, axis=-1))
    o_c, bk_new, bv_new = chunk_band_mixer(q_c, k_c, v_c, pos, bk_c, bv_c, band_rel_bias)
    o_d, s5re_new, s5im_new = s5_mixer(u_d, s5re_c, s5im_c, s5_a_re, s5_a_im, s5_log_dt, s5_b_re, s5_b_im,
                                       s5_c_re, s5_c_im, s5_d, s5_w_glu, s5_b_glu)
    outs = (o_a, o_b, o_c.reshape(bsz, seq, MIX_W), o_d)
    gate = jax.nn.sigmoid(gates.reshape(bsz, seq, N_BRANCH, D_MODEL))
    merged = sum(gate[:, :, i] * (outs[i] @ w_branch[i]) for i in range(N_BRANCH))
    x = x + rms_norm(merged @ w_out, norm_g[3])
    x = x + 0.5 * rms_norm(swiglu(rms_norm(x, norm_g[4]), ffn_w_up[1], ffn_w_down[1]), norm_g[5])
    return x, (ckv_new, kr_new, h_new, conv_new, bk_new, bv_new, s5re_new, s5im_new)


def setup_inputs(seed: int = 0) -> dict:
    key = jax.random.key(seed)
    ks = iter(jax.random.split(key, 48))
    f32 = jnp.float32

    def nrm(shape, scale):
        return scale * jax.random.normal(next(ks), shape, f32)

    def unif(shape, lo, hi):
        return jax.random.uniform(next(ks), shape, f32, minval=lo, maxval=hi)

    c_win = min(BAND_CHUNKS * CHUNK, PAST_LEN)
    lru_a = unif((DEPTH, B_WIDTH), 0.9, 0.999) ** (1.0 / LRU_C)
    s5_idx = jnp.arange(S5_N, dtype=f32)
    return {
        'x_prompt': nrm((BATCH, SEQ, D_MODEL), 1.0),
        'x_sample': nrm((DEC_BATCH, DEC_SEQ, D_MODEL), 1.0),
        'cache_mla_ckv': nrm((DEPTH, DEC_BATCH, PAST_LEN, KV_LORA), 1.0),
        'cache_mla_krope': nrm((DEPTH, DEC_BATCH, PAST_LEN, A_ROPE), 1.0),
        'state_lru_h': nrm((DEPTH, DEC_BATCH, B_WIDTH), 0.5),
        'state_lru_conv': nrm((DEPTH, DEC_BATCH, CONV_W - 1, B_WIDTH), 1.0),
        'cache_band_k': nrm((DEPTH, DEC_BATCH, c_win, C_HEADS, C_HDIM), 1.0),
        'cache_band_v': nrm((DEPTH, DEC_BATCH, c_win, C_HEADS, C_HDIM), 1.0),
        'state_s5_re': nrm((DEPTH, DEC_BATCH, S5_GROUPS, S5_N), 0.1),
        'state_s5_im': nrm((DEPTH, DEC_BATCH, S5_GROUPS, S5_N), 0.1),
        'norm_g': 1.0 + nrm((DEPTH, 6, D_MODEL), 0.01),
        'ffn_w_up': nrm((DEPTH, 2, D_MODEL, 2 * D_FF), D_MODEL ** -0.5),
        'ffn_w_down': nrm((DEPTH, 2, D_FF, D_MODEL), D_FF ** -0.5),
        'w_in': nrm((DEPTH, D_MODEL, IN_WIDTH), D_MODEL ** -0.5),
        'mla_q_norm': 1.0 + nrm((DEPTH, Q_LORA), 0.01),
        'mla_kv_norm': 1.0 + nrm((DEPTH, KV_LORA), 0.01),
        'mla_w_uq': nrm((DEPTH, Q_LORA, A_HEADS * (A_NOPE + A_ROPE)), Q_LORA ** -0.5),
        'mla_w_ukv': nrm((DEPTH, KV_LORA, A_HEADS * (A_NOPE + A_VDIM)), KV_LORA ** -0.5),
        'lru_conv_w': nrm((DEPTH, CONV_W, B_WIDTH), CONV_W ** -0.5),
        'lru_conv_b': nrm((DEPTH, B_WIDTH), 0.01),
        'lru_w_a': nrm((DEPTH, B_BLOCKS, B_BW, B_BW), B_BW ** -0.5),
        'lru_b_a': nrm((DEPTH, B_WIDTH), 0.01),
        'lru_w_x': nrm((DEPTH, B_BLOCKS, B_BW, B_BW), B_BW ** -0.5),
        'lru_b_x': nrm((DEPTH, B_WIDTH), 0.01),
        'lru_lambda': jnp.log(lru_a) - jnp.log1p(-lru_a),
        'band_rel_bias': nrm((DEPTH, C_HEADS, 2 * REL_CLIP + 1), 0.1),
        's5_a_re': -0.5 + nrm((DEPTH, S5_GROUPS, S5_N), 0.01),
        's5_a_im': math.pi * s5_idx + nrm((DEPTH, S5_GROUPS, S5_N), 0.01),
        's5_log_dt': unif((DEPTH, S5_GROUPS), math.log(1e-3), math.log(1e-1)),
        's5_b_re': nrm((DEPTH, S5_GROUPS, S5_N, S5_GC), (2 * S5_GC) ** -0.5),
        's5_b_im': nrm((DEPTH, S5_GROUPS, S5_N, S5_GC), (2 * S5_GC) ** -0.5),
        's5_c_re': nrm((DEPTH, S5_GROUPS, S5_GC, S5_N), (2 * S5_N) ** -0.5),
        's5_c_im': nrm((DEPTH, S5_GROUPS, S5_GC, S5_N), (2 * S5_N) ** -0.5),
        's5_d': nrm((DEPTH, D_WIDTH), 0.5),
        's5_w_glu': nrm((DEPTH, D_WIDTH, D_WIDTH), D_WIDTH ** -0.5),
        's5_b_glu': nrm((DEPTH, D_WIDTH), 0.01),
        'w_branch': nrm((DEPTH, N_BRANCH, MIX_W, D_MODEL), MIX_W ** -0.5),
        'w_out': nrm((DEPTH, D_MODEL, D_MODEL), D_MODEL ** -0.5),
    }


def reference(x_prompt, x_sample, cache_mla_ckv, cache_mla_krope, state_lru_h, state_lru_conv,
              cache_band_k, cache_band_v, state_s5_re, state_s5_im,
              norm_g, ffn_w_up, ffn_w_down, w_in, mla_q_norm, mla_kv_norm, mla_w_uq, mla_w_ukv,
              lru_conv_w, lru_conv_b, lru_w_a, lru_b_a, lru_w_x, lru_b_x, lru_lambda,
              band_rel_bias, s5_a_re, s5_a_im, s5_log_dt, s5_b_re, s5_b_im, s5_c_re, s5_c_im,
              s5_d, s5_w_glu, s5_b_glu, w_branch, w_out):
    past = cache_mla_ckv.shape[2]
    pos_p = jnp.arange(x_prompt.shape[1], dtype=jnp.int32)
    pos_s = past + jnp.arange(x_sample.shape[1], dtype=jnp.int32)
    yp, ys = x_prompt, x_sample
    new_p, new_s = [], []
    for l in range(DEPTH):
        lp = (norm_g[l], ffn_w_up[l], ffn_w_down[l], w_in[l], mla_q_norm[l], mla_kv_norm[l],
              mla_w_uq[l], mla_w_ukv[l], lru_conv_w[l], lru_conv_b[l], lru_w_a[l], lru_b_a[l],
              lru_w_x[l], lru_b_x[l], lru_lambda[l], band_rel_bias[l], s5_a_re[l], s5_a_im[l],
              s5_log_dt[l], s5_b_re[l], s5_b_im[l], s5_c_re[l], s5_c_im[l], s5_d[l],
              s5_w_glu[l], s5_b_glu[l], w_branch[l], w_out[l])
        yp, st_p = trunk_layer(yp, pos_p, None, None, None, None, None, None, None, None, *lp)
        ys, st_s = trunk_layer(ys, pos_s, cache_mla_ckv[l], cache_mla_krope[l], state_lru_h[l],
                               state_lru_conv[l], cache_band_k[l], cache_band_v[l],
                               state_s5_re[l], state_s5_im[l], *lp)
        new_p.append(st_p)
        new_s.append(st_s)

    def stacked(states, i):
        return jnp.stack([st[i] for st in states], axis=0)

    return (yp, ys,
            stacked(new_p, 0), stacked(new_s, 0),
            stacked(new_p, 1), stacked(new_s, 1),
            stacked(new_p, 2), stacked(new_s, 2),
            stacked(new_p, 3), stacked(new_s, 3),
            stacked(new_p, 4), stacked(new_s, 4),
            stacked(new_p, 5), stacked(new_s, 5),
            stacked(new_p, 6), stacked(new_s, 6),
            stacked(new_p, 7), stacked(new_s, 7))
```

```python
import functools
import math

import jax
import jax.numpy as jnp
import numpy as np
from jax import lax
from jax.experimental import pallas as pl
from jax.experimental.pallas import tpu as pltpu

F32 = jnp.float32
BF16 = jnp.bfloat16

CHUNK = 64
BAND_CHUNKS = 8
EPS = 1e-6
ROPE_THETA = 10000.0
LRU_C = 8.0
ROPE_HALF = 32
MASK_VALUE = -1e30
LANE = 128
SUBLANE = 8
VMEM_LIMIT = 56 * 1024 * 1024


def _cparams(*sem):
    return pltpu.CompilerParams(dimension_semantics=sem, vmem_limit_bytes=VMEM_LIMIT)


def _round_up(x, m):
    return (x + m - 1) // m * m


def _tile(dim, pref):
    if dim <= pref:
        return dim
    t = pref
    while dim % t:
        t //= 2
    assert t >= SUBLANE, (dim, pref)
    return t


def _rmsnorm_kernel(x_ref, g_ref, o_ref):
    x = x_ref[...]
    ms = jnp.mean(x * x, axis=-1, keepdims=True)
    o_ref[...] = (x * lax.rsqrt(ms + EPS) * g_ref[...]).astype(o_ref.dtype)


def rmsnorm(x, g, out_dtype=BF16):
    t, d = x.shape
    tm = _tile(t, 256)
    return pl.pallas_call(
        _rmsnorm_kernel,
        grid=(t // tm,),
        in_specs=[pl.BlockSpec((tm, d), lambda i: (i, 0)), pl.BlockSpec((1, d), lambda i: (0, 0))],
        out_specs=pl.BlockSpec((tm, d), lambda i: (i, 0)),
        out_shape=jax.ShapeDtypeStruct((t, d), out_dtype),
        compiler_params=_cparams("parallel"),
        name="rmsnorm",
    )(x, g.reshape(1, d))


def _add_rmsnorm_kernel(x_ref, y_ref, g_ref, o_ref, *, scale):
    y = y_ref[...]
    ms = jnp.mean(y * y, axis=-1, keepdims=True)
    o_ref[...] = x_ref[...] + scale * (y * lax.rsqrt(ms + EPS) * g_ref[...])


def add_rmsnorm(x, y, g, scale):
    t, d = x.shape
    tm = _tile(t, 256)
    return pl.pallas_call(
        functools.partial(_add_rmsnorm_kernel, scale=scale),
        grid=(t // tm,),
        in_specs=[pl.BlockSpec((tm, d), lambda i: (i, 0)), pl.BlockSpec((tm, d), lambda i: (i, 0)),
                  pl.BlockSpec((1, d), lambda i: (0, 0))],
        out_specs=pl.BlockSpec((tm, d), lambda i: (i, 0)),
        out_shape=jax.ShapeDtypeStruct((t, d), F32),
        compiler_params=_cparams("parallel"),
        name="add_rmsnorm",
    )(x, y, g.reshape(1, d))


def _k_tile(k):
    if k <= 4096:
        return k
    for parts in range(2, k // LANE + 1):
        if k % parts == 0 and (k // parts) % LANE == 0 and k // parts <= 4096:
            return k // parts
    raise ValueError(k)


def _mm_kernel(*refs, nk, n_extra, epilogue):
    a_ref, b_ref = refs[0], refs[1]
    extra = refs[2:2 + n_extra]
    o_ref = refs[2 + n_extra]
    if nk == 1:
        acc = jnp.dot(a_ref[...], b_ref[...], preferred_element_type=F32)
        o_ref[...] = epilogue(acc, *extra).astype(o_ref.dtype)
        return
    acc_ref = refs[3 + n_extra]
    k = pl.program_id(2)

    @pl.when(k == 0)
    def _():
        acc_ref[...] = jnp.zeros_like(acc_ref)

    acc_ref[...] += jnp.dot(a_ref[...], b_ref[...], preferred_element_type=F32)

    @pl.when(k == nk - 1)
    def _():
        o_ref[...] = epilogue(acc_ref[...], *extra).astype(o_ref.dtype)


def _ep_none(acc):
    return acc


def _ep_sigmoid(acc):
    return jax.nn.sigmoid(acc)


def _ep_glu(acc, z_ref, b_ref):
    z = z_ref[...]
    return z * jax.nn.sigmoid(acc + b_ref[...])


def _ep_rope_q(acc, cos_ref, sin_ref, *, scale, half):
    cos, sin = cos_ref[...], sin_ref[...]
    lane = lax.broadcasted_iota(jnp.int32, cos.shape, 1)
    outs = []
    for h in range(acc.shape[1] // (2 * LANE)):
        nope = acc[:, h * 2 * LANE:h * 2 * LANE + LANE]
        r = acc[:, h * 2 * LANE + LANE:(h + 1) * 2 * LANE]
        swapped = jnp.where(lane < half, pltpu.roll(r, LANE - half, axis=1), pltpu.roll(r, half, axis=1))
        outs += [nope * scale, (r * cos + swapped * sin) * scale]
    return jnp.concatenate(outs, axis=1)


def matmul(a, b, *, out_dtype, m=None, tm=1024, tn=1024, epilogue=_ep_none, extra=(), name="matmul"):
    m = a.shape[0] if m is None else m
    k, n = b.shape
    assert a.shape[1] == k
    tm, tn, tk = _tile(m, tm), _tile(n, tn), _k_tile(k)
    nk = k // tk
    if nk == 1:
        grid = (m // tm, n // tn)
        wrap = lambda f: f
        a_spec = pl.BlockSpec((tm, k), lambda i, j: (i, 0))
        b_spec = pl.BlockSpec((k, tn), lambda i, j: (0, j))
        o_spec = pl.BlockSpec((tm, tn), lambda i, j: (i, j))
        scratch = []
        sem = ("parallel", "parallel")
    else:
        grid = (m // tm, n // tn, nk)
        wrap = lambda f: (lambda i, j, kk: f(i, j))
        a_spec = pl.BlockSpec((tm, tk), lambda i, j, kk: (i, kk))
        b_spec = pl.BlockSpec((tk, tn), lambda i, j, kk: (kk, j))
        o_spec = pl.BlockSpec((tm, tn), lambda i, j, kk: (i, j))
        scratch = [pltpu.VMEM((tm, tn), F32)]
        sem = ("parallel", "parallel", "arbitrary")
    extra_specs = [pl.BlockSpec(bs, wrap(im)) for _, bs, im in extra]
    return pl.pallas_call(
        functools.partial(_mm_kernel, nk=nk, n_extra=len(extra), epilogue=epilogue),
        grid=grid,
        in_specs=[a_spec, b_spec] + extra_specs,
        out_specs=o_spec,
        out_shape=jax.ShapeDtypeStruct((m, n), out_dtype),
        scratch_shapes=scratch,
        compiler_params=_cparams(*sem),
        name=name,
    )(a, b, *[e[0] for e in extra])


def _swiglu_kernel(a_ref, wg_ref, wu_ref, o_ref):
    a = a_ref[...]
    gate = jnp.dot(a, wg_ref[...], preferred_element_type=F32)
    up = jnp.dot(a, wu_ref[...], preferred_element_type=F32)
    o_ref[...] = (jax.nn.silu(gate) * up).astype(o_ref.dtype)


def swiglu_up(a, w_up, ffp):
    t, d = a.shape
    tm, tn = _tile(t, 1024), _tile(ffp, 512)
    nj = ffp // tn
    return pl.pallas_call(
        _swiglu_kernel,
        grid=(t // tm, nj),
        in_specs=[pl.BlockSpec((tm, d), lambda i, j: (i, 0)),
                  pl.BlockSpec((d, tn), lambda i, j: (0, j)),
                  pl.BlockSpec((d, tn), lambda i, j: (0, j + nj))],
        out_specs=pl.BlockSpec((tm, tn), lambda i, j: (i, j)),
        out_shape=jax.ShapeDtypeStruct((t, ffp), BF16),
        compiler_params=_cparams("parallel", "parallel"),
        name="swiglu_up",
    )(a, w_up, w_up)


def _merge_kernel(oa_ref, ob_ref, oc_ref, od_ref, wb_ref, ga_ref, gb_ref, gc_ref, gd_ref, o_ref):
    total = None
    for i, (o_r, g_r) in enumerate(((oa_ref, ga_ref), (ob_ref, gb_ref), (oc_ref, gc_ref), (od_ref, gd_ref))):
        part = g_r[...].astype(F32) * jnp.dot(o_r[...], wb_ref[i], preferred_element_type=F32)
        total = part if total is None else total + part
    o_ref[...] = total.astype(o_ref.dtype)


def merge_branches(outs, w_branch, gates):
    t, w = outs[0].shape
    d = w_branch.shape[2]
    tm, tn = _tile(t, 512), _tile(d, 512)
    nj = d // tn
    o_spec = pl.BlockSpec((tm, w), lambda i, j: (i, 0))
    g_specs = [pl.BlockSpec((tm, tn), functools.partial(lambda i, j, b: (i, j + b * nj), b=b)) for b in range(4)]
    return pl.pallas_call(
        _merge_kernel,
        grid=(t // tm, nj),
        in_specs=[o_spec] * 4 + [pl.BlockSpec((4, w, tn), lambda i, j: (0, 0, j))] + g_specs,
        out_specs=pl.BlockSpec((tm, tn), lambda i, j: (i, j)),
        out_shape=jax.ShapeDtypeStruct((t, d), BF16),
        compiler_params=_cparams("parallel", "parallel"),
        name="merge_branches",
    )(*outs, w_branch, gates, gates, gates, gates)


def _mla_prep_kernel(c_ref, qn_ref, kvn_ref, cos_ref, sin_ref, cq_ref, ckv_ref, ckvb_ref, kr_ref, krb_ref,
                     *, q_lora, kv_lora, half):
    c_q = c_ref[:, :q_lora]
    ms = jnp.mean(c_q * c_q, axis=-1, keepdims=True)
    cq_ref[...] = (c_q * lax.rsqrt(ms + EPS) * qn_ref[...]).astype(cq_ref.dtype)
    c_kv = c_ref[:, q_lora:q_lora + kv_lora]
    ms = jnp.mean(c_kv * c_kv, axis=-1, keepdims=True)
    ckv = c_kv * lax.rsqrt(ms + EPS) * kvn_ref[...]
    ckv_ref[...] = ckv
    ckvb_ref[...] = ckv.astype(ckvb_ref.dtype)
    r = c_ref[:, q_lora + kv_lora:q_lora + kv_lora + LANE]
    lane = lax.broadcasted_iota(jnp.int32, r.shape, 1)
    swapped = jnp.where(lane < half, pltpu.roll(r, LANE - half, axis=1), pltpu.roll(r, half, axis=1))
    kr = r * cos_ref[...] + swapped * sin_ref[...]
    kr_ref[...] = kr
    krb_ref[...] = kr.astype(krb_ref.dtype)


def mla_prep(segf, q_norm, kv_norm, cos, sin, cpad):
    t = segf.shape[0]
    q_lora, kv_lora = q_norm.shape[0], kv_norm.shape[0]
    tm = _tile(t, 256)
    row = lambda w: pl.BlockSpec((tm, w), lambda i: (i, 0))
    const = lambda w: pl.BlockSpec((1, w), lambda i: (0, 0))
    return pl.pallas_call(
        functools.partial(_mla_prep_kernel, q_lora=q_lora, kv_lora=kv_lora, half=ROPE_HALF),
        grid=(t // tm,),
        in_specs=[row(cpad), const(q_lora), const(kv_lora), row(LANE), row(LANE)],
        out_specs=[row(q_lora), row(kv_lora), row(kv_lora), row(LANE), row(LANE)],
        out_shape=[jax.ShapeDtypeStruct((t, q_lora), BF16), jax.ShapeDtypeStruct((t, kv_lora), F32),
                   jax.ShapeDtypeStruct((t, kv_lora), BF16), jax.ShapeDtypeStruct((t, LANE), F32),
                   jax.ShapeDtypeStruct((t, LANE), BF16)],
        compiler_params=_cparams("parallel"),
        name="mla_prep",
    )(segf, q_norm.reshape(1, -1), kv_norm.reshape(1, -1), cos, sin)


def _mla_flash_kernel(qi_ref, ki_ref, q_ref, kv_ref, kr_ref, o_ref, m_ref, l_ref, acc_ref, *, heads, tq, tk):
    s_id = pl.program_id(1)
    qi, ki = qi_ref[s_id], ki_ref[s_id]

    @pl.when(ki == 0)
    def _():
        m_ref[...] = jnp.full_like(m_ref, -jnp.inf)
        l_ref[...] = jnp.zeros_like(l_ref)
        acc_ref[...] = jnp.zeros_like(acc_ref)

    def step(masked):
        kr = kr_ref[...]
        if masked:
            rc = lax.broadcasted_iota(jnp.int32, (tq, tk), 0) // CHUNK
            cc = lax.broadcasted_iota(jnp.int32, (tq, tk), 1) // CHUNK
            visible = cc <= rc
        for h in range(heads):
            q = q_ref[:, h * 2 * LANE:(h + 1) * 2 * LANE]
            k = jnp.concatenate([kv_ref[:, h * 2 * LANE:h * 2 * LANE + LANE], kr], axis=1)
            v = kv_ref[:, h * 2 * LANE + LANE:(h + 1) * 2 * LANE]
            s = lax.dot_general(q, k, (((1,), (1,)), ((), ())), preferred_element_type=F32)
            if masked:
                s = jnp.where(visible, s, MASK_VALUE)
            m_prev = m_ref[h]
            m_new = jnp.maximum(m_prev, jnp.max(s, axis=-1, keepdims=True))
            alpha = jnp.exp(m_prev - m_new)
            p = jnp.exp(s - m_new)
            l_ref[h] = alpha * l_ref[h] + jnp.sum(p, axis=-1, keepdims=True)
            m_ref[h] = m_new
            pv = jnp.dot(p.astype(v.dtype), v, preferred_element_type=F32)
            acc_ref[:, h * LANE:(h + 1) * LANE] = alpha * acc_ref[:, h * LANE:(h + 1) * LANE] + pv

    @pl.when(ki < qi)
    def _():
        step(False)

    @pl.when(ki == qi)
    def _():
        step(True)
        for h in range(heads):
            o_ref[:, h * LANE:(h + 1) * LANE] = (
                acc_ref[:, h * LANE:(h + 1) * LANE] / l_ref[h]).astype(o_ref.dtype)


def mla_flash(q, kv, kr, seq, n_heads, *, tile=512, heads_per_step=4):
    tq = tk = _tile(seq, tile)
    hg = min(heads_per_step, n_heads)
    assert n_heads % hg == 0 and tq % CHUNK == 0
    nq = seq // tq
    pairs = [(i, j) for i in range(nq) for j in range(i + 1)]
    qi_arr = jnp.asarray(np.array([p[0] for p in pairs], np.int32))
    ki_arr = jnp.asarray(np.array([p[1] for p in pairs], np.int32))
    grid_spec = pltpu.PrefetchScalarGridSpec(
        num_scalar_prefetch=2,
        grid=(n_heads // hg, len(pairs)),
        in_specs=[pl.BlockSpec((tq, hg * 2 * LANE), lambda g, s, qi, ki: (qi[s], g)),
                  pl.BlockSpec((tk, hg * 2 * LANE), lambda g, s, qi, ki: (ki[s], g)),
                  pl.BlockSpec((tk, LANE), lambda g, s, qi, ki: (ki[s], 0))],
        out_specs=pl.BlockSpec((tq, hg * LANE), lambda g, s, qi, ki: (qi[s], g)),
        scratch_shapes=[pltpu.VMEM((hg, tq, 1), F32), pltpu.VMEM((hg, tq, 1), F32),
                        pltpu.VMEM((tq, hg * LANE), F32)],
    )
    return pl.pallas_call(
        functools.partial(_mla_flash_kernel, heads=hg, tq=tq, tk=tk),
        grid_spec=grid_spec,
        out_shape=jax.ShapeDtypeStruct((seq, n_heads * LANE), BF16),
        compiler_params=_cparams("parallel", "arbitrary"),
        name="mla_flash",
    )(qi_arr, ki_arr, q, kv, kr)


def _mla_sample_kernel(q_ref, kv_ref, kr_ref, o_ref, *, heads, n_keys, past):
    sq, kp = q_ref.shape[0], kv_ref.shape[0]
    kr = kr_ref[...]
    qpos = past + lax.broadcasted_iota(jnp.int32, (sq, kp), 0)
    kpos = lax.broadcasted_iota(jnp.int32, (sq, kp), 1)
    visible = (kpos < n_keys) & (kpos // CHUNK <= qpos // CHUNK)
    for h in range(heads):
        q = q_ref[:, h * 2 * LANE:(h + 1) * 2 * LANE]
        k = jnp.concatenate([kv_ref[:, h * 2 * LANE:h * 2 * LANE + LANE], kr], axis=1)
        v = kv_ref[:, h * 2 * LANE + LANE:(h + 1) * 2 * LANE]
        s = lax.dot_general(q, k, (((1,), (1,)), ((), ())), preferred_element_type=F32)
        s = jnp.where(visible, s, MASK_VALUE)
        p = jnp.exp(s - jnp.max(s, axis=-1, keepdims=True))
        denom = jnp.sum(p, axis=-1, keepdims=True)
        pv = jnp.dot(p.astype(v.dtype), v, preferred_element_type=F32)
        o_ref[:, h * LANE:(h + 1) * LANE] = (pv / denom).astype(o_ref.dtype)


def mla_sample(q, kv, kr, row0, bsz, sq, n_heads, n_keys, past):
    kp = kv.shape[0] // bsz
    assert row0 % sq == 0
    return pl.pallas_call(
        functools.partial(_mla_sample_kernel, heads=n_heads, n_keys=n_keys, past=past),
        grid=(bsz,),
        in_specs=[pl.BlockSpec((sq, n_heads * 2 * LANE), lambda b: (row0 // sq + b, 0)),
                  pl.BlockSpec((kp, n_heads * 2 * LANE), lambda b: (b, 0)),
                  pl.BlockSpec((kp, LANE), lambda b: (b, 0))],
        out_specs=pl.BlockSpec((sq, n_heads * LANE), lambda b: (b, 0)),
        out_shape=jax.ShapeDtypeStruct((bsz * sq, n_heads * LANE), BF16),
        compiler_params=_cparams("parallel"),
        name="mla_sample",
    )(q, kv, kr)


def _band_kernel(q_ref, kp_ref, ko_ref, vp_ref, vo_ref, bp_ref, bo_ref, o_ref, *, scale, mask_first_prev):
    q = (q_ref[...] * scale).astype(BF16)
    dn = (((1,), (1,)), ((), ()))
    sp = lax.dot_general(q, kp_ref[...].astype(BF16), dn, preferred_element_type=F32) + bp_ref[0]
    so = lax.dot_general(q, ko_ref[...].astype(BF16), dn, preferred_element_type=F32) + bo_ref[0]
    if mask_first_prev:
        sp = jnp.where(pl.program_id(1) > 0, sp, MASK_VALUE)
    m = jnp.maximum(jnp.max(sp, axis=-1, keepdims=True), jnp.max(so, axis=-1, keepdims=True))
    pp, po = jnp.exp(sp - m), jnp.exp(so - m)
    denom = jnp.sum(pp, axis=-1, keepdims=True) + jnp.sum(po, axis=-1, keepdims=True)
    pv = (jnp.dot(pp.astype(BF16), vp_ref[...].astype(BF16), preferred_element_type=F32)
          + jnp.dot(po.astype(BF16), vo_ref[...].astype(BF16), preferred_element_type=F32))
    o_ref[...] = (pv / denom).astype(o_ref.dtype)


def band_attention(q_arr, q_row0, q_col0, kp_arr, kp_map, vp_arr, vp_map, ko_arr, ko_map, vo_arr, vo_map,
                   bias_p, bias_o, *, n_heads, n_blocks, tq, prev, own, mask_first_prev):
    hd = LANE
    assert q_row0 % tq == 0
    return pl.pallas_call(
        functools.partial(_band_kernel, scale=hd ** -0.5, mask_first_prev=mask_first_prev),
        grid=(n_heads, n_blocks),
        in_specs=[pl.BlockSpec((tq, hd), lambda h, i: (q_row0 // tq + i, q_col0 + h)),
                  pl.BlockSpec((prev, hd), lambda h, i: kp_map(i, h)),
                  pl.BlockSpec((own, hd), lambda h, i: ko_map(i, h)),
                  pl.BlockSpec((prev, hd), lambda h, i: vp_map(i, h)),
                  pl.BlockSpec((own, hd), lambda h, i: vo_map(i, h)),
                  pl.BlockSpec((1, tq, prev), lambda h, i: (h, 0, 0)),
                  pl.BlockSpec((1, tq, own), lambda h, i: (h, 0, 0))],
        out_specs=pl.BlockSpec((tq, hd), lambda h, i: (i, h)),
        out_shape=jax.ShapeDtypeStruct((n_blocks * tq, n_heads * hd), BF16),
        compiler_params=_cparams("parallel", "arbitrary"),
        name="band_attention",
    )(q_arr, kp_arr, ko_arr, vp_arr, vo_arr, bias_p, bias_o)


def band_bias_tables(rel_bias, tq, prev, own, own_valid):
    clip = (rel_bias.shape[1] - 1) // 2
    qr = np.arange(tq)[:, None]

    def table(krel, valid):
        rel = np.clip(qr - krel, -clip, clip) + clip
        dc = qr // CHUNK - np.floor_divide(krel, CHUNK)
        vis = (dc >= 0) & (dc <= BAND_CHUNKS) & valid
        return jnp.where(jnp.asarray(vis)[None], jnp.take(rel_bias, jnp.asarray(rel), axis=1), MASK_VALUE)

    kp = np.arange(-prev, 0)[None, :]
    ko = np.arange(own)[None, :]
    return table(kp, np.ones_like(kp, bool)), table(ko, ko < own_valid)


def _rglru_kernel(u_ref, h0_ref, c0_ref, cw_ref, cb_ref, wa_ref, ba_ref, wx_ref, bx_ref, lam_ref,
                  o_ref, hl_ref, prev_ref, h_ref, a_ref, b_ref, *, n_blocks, bw):
    t = pl.program_id(1)
    tb, width = u_ref.shape

    @pl.when(t == 0)
    def _():
        prev_ref[...] = c0_ref[0]
        h_ref[...] = h0_ref[0]

    u = u_ref[...]
    cw = cw_ref[...]
    nw = 4
    xc = cb_ref[...] + cw[nw - 1:nw, :] * u
    for s in range(1, nw):
        xc = xc + cw[nw - 1 - s:nw - s, :] * pltpu.roll(u, s, axis=0)
    head = jnp.concatenate([prev_ref[...], u[:SUBLANE, :]], axis=0)
    xh = cb_ref[...] + cw[nw - 1:nw, :] * u[:SUBLANE, :]
    for s in range(1, nw):
        xh = xh + cw[nw - 1 - s:nw - s, :] * pltpu.roll(head, s, axis=0)[SUBLANE:, :]
    a_ref[:SUBLANE, :] = xh
    a_ref[SUBLANE:, :] = xc[SUBLANE:, :]
    xc = a_ref[...]
    prev_ref[...] = u[tb - SUBLANE:, :]

    neg_c_sp = -LRU_C * jax.nn.softplus(-lam_ref[...])
    for n in range(n_blocks):
        sl = slice(n * bw, (n + 1) * bw)
        xb = xc[:, sl].astype(BF16)
        r = jax.nn.sigmoid(jnp.dot(xb, wa_ref[n], preferred_element_type=F32) + ba_ref[:, sl])
        i = jax.nn.sigmoid(jnp.dot(xb, wx_ref[n], preferred_element_type=F32) + bx_ref[:, sl])
        log_a = neg_c_sp[:, sl] * r
        a = jnp.exp(log_a)
        a_ref[:, sl] = a
        b_ref[:, sl] = jnp.sqrt(-jnp.tanh(log_a) * (a * a + 1.0)) * (i * xc[:, sl])

    def row(j, h):
        h = a_ref[pl.ds(j, 1), :] * h + b_ref[pl.ds(j, 1), :]
        b_ref[pl.ds(j, 1), :] = h
        return h

    h = lax.fori_loop(0, tb, row, h_ref[...])
    h_ref[...] = h
    o_ref[...] = b_ref[...].astype(o_ref.dtype)
    hl_ref[0] = h


def rglru(u_arr, row0, col0, n_seq, seq, h0, conv0, conv_w, conv_b, w_a, b_a, w_x, b_x, lam):
    width = h0.shape[-1]
    n_blocks, bw = w_a.shape[0], w_a.shape[1]
    tb = _tile(seq, 256)
    nt = seq // tb
    assert row0 % tb == 0 and tb >= 2 * SUBLANE and conv_w.shape[0] == 4
    conv0p = jnp.concatenate([jnp.zeros((n_seq, SUBLANE - conv0.shape[1], width), F32), conv0], axis=1)
    cwp = jnp.concatenate([conv_w, jnp.zeros((SUBLANE - conv_w.shape[0], width), F32)], axis=0)
    vec = lambda: pl.BlockSpec((1, width), lambda s, t: (0, 0))
    wspec = lambda: pl.BlockSpec((n_blocks, bw, bw), lambda s, t: (0, 0, 0))
    out, hl = pl.pallas_call(
        functools.partial(_rglru_kernel, n_blocks=n_blocks, bw=bw),
        grid=(n_seq, nt),
        in_specs=[pl.BlockSpec((tb, width), lambda s, t: (row0 // tb + s * nt + t, col0)),
                  pl.BlockSpec((1, 1, width), lambda s, t: (s, 0, 0)),
                  pl.BlockSpec((1, SUBLANE, width), lambda s, t: (s, 0, 0)),
                  pl.BlockSpec((SUBLANE, width), lambda s, t: (0, 0)),
                  vec(), wspec(), vec(), wspec(), vec(), vec()],
        out_specs=[pl.BlockSpec((tb, width), lambda s, t: (s * nt + t, 0)),
                   pl.BlockSpec((1, 1, width), lambda s, t: (s, 0, 0))],
        out_shape=[jax.ShapeDtypeStruct((n_seq * seq, width), BF16),
                   jax.ShapeDtypeStruct((n_seq, 1, width), F32)],
        scratch_shapes=[pltpu.VMEM((SUBLANE, width), F32), pltpu.VMEM((1, width), F32),
                        pltpu.VMEM((tb, width), F32), pltpu.VMEM((tb, width), F32)],
        compiler_params=_cparams("parallel", "arbitrary"),
        name="rglru",
    )(u_arr, h0.reshape(n_seq, 1, width), conv0p, cwp, conv_b.reshape(1, -1), w_a, b_a.reshape(1, -1),
      w_x, b_x.reshape(1, -1), lam.reshape(1, -1))
    return out, hl.reshape(n_seq, width)


def _s5_kernel(u_ref, x0_ref, are_ref, aim_ref, wb_ref, wc_ref, d_ref, z_ref, xl_ref, xs_ref, xc_ref,
               *, n_kb, strip):
    t = pl.program_id(1)
    tb, width = u_ref.shape
    half = xs_ref.shape[1] // 2
    kw = width // n_kb
    sw = half // n_kb

    @pl.when(t == 0)
    def _():
        xc_ref[...] = x0_ref[0]

    u = u_ref[...]
    ub = u.astype(BF16)
    for kb in range(n_kb):
        bu = jnp.dot(ub[:, kb * kw:(kb + 1) * kw], wb_ref[kb], preferred_element_type=F32)
        xs_ref[:, kb * sw:(kb + 1) * sw] = bu[:, :sw]
        xs_ref[:, half + kb * sw:half + (kb + 1) * sw] = bu[:, sw:]

    for c in range(half // strip):
        re = slice(c * strip, (c + 1) * strip)
        im = slice(half + c * strip, half + (c + 1) * strip)
        a_re, a_im = are_ref[:, re], aim_ref[:, re]

        def row(j, carry):
            x_re, x_im = carry
            n_re = a_re * x_re - a_im * x_im + xs_ref[pl.ds(j, 1), re]
            n_im = a_re * x_im + a_im * x_re + xs_ref[pl.ds(j, 1), im]
            xs_ref[pl.ds(j, 1), re] = n_re
            xs_ref[pl.ds(j, 1), im] = n_im
            return n_re, n_im

        x_re, x_im = lax.fori_loop(0, tb, row, (xc_ref[:, re], xc_ref[:, im]))
        xc_ref[:, re] = x_re
        xc_ref[:, im] = x_im

    xl_ref[0] = xc_ref[...]
    for kb in range(n_kb):
        xb = jnp.concatenate([xs_ref[:, kb * sw:(kb + 1) * sw],
                              xs_ref[:, half + kb * sw:half + (kb + 1) * sw]], axis=1).astype(BF16)
        y = jnp.dot(xb, wc_ref[kb], preferred_element_type=F32)
        cs = slice(kb * kw, (kb + 1) * kw)
        z_ref[:, cs] = jax.nn.gelu(y + d_ref[:, cs] * u[:, cs])


def s5_scan(u_arr, row0, col0, n_seq, seq, x0, a_re, a_im, wb, wc, d_skip):
    n_kb, kw, two_sw = wb.shape
    width = n_kb * kw
    half = n_kb * two_sw // 2
    tb = _tile(seq, 128)
    nt = seq // tb
    assert row0 % tb == 0
    z, xl = pl.pallas_call(
        functools.partial(_s5_kernel, n_kb=n_kb, strip=min(512, half)),
        grid=(n_seq, nt),
        in_specs=[pl.BlockSpec((tb, width), lambda s, t: (row0 // tb + s * nt + t, col0)),
                  pl.BlockSpec((1, 1, 2 * half), lambda s, t: (s, 0, 0)),
                  pl.BlockSpec((1, half), lambda s, t: (0, 0)),
                  pl.BlockSpec((1, half), lambda s, t: (0, 0)),
                  pl.BlockSpec(wb.shape, lambda s, t: (0, 0, 0)),
                  pl.BlockSpec(wc.shape, lambda s, t: (0, 0, 0)),
                  pl.BlockSpec((1, width), lambda s, t: (0, 0))],
        out_specs=[pl.BlockSpec((tb, width), lambda s, t: (s * nt + t, 0)),
                   pl.BlockSpec((1, 1, 2 * half), lambda s, t: (s, 0, 0))],
        out_shape=[jax.ShapeDtypeStruct((n_seq * seq, width), F32),
                   jax.ShapeDtypeStruct((n_seq, 1, 2 * half), F32)],
        scratch_shapes=[pltpu.VMEM((tb, 2 * half), F32), pltpu.VMEM((1, 2 * half), F32)],
        compiler_params=_cparams("parallel", "arbitrary"),
        name="s5_scan",
    )(u_arr, x0, a_re, a_im, wb, wc, d_skip.reshape(1, -1))
    return z, xl.reshape(n_seq, 2 * half)


def s5_params(a_re, a_im, log_dt, b_re, b_im, c_re, c_im, kw):
    g, n, gc = b_re.shape
    gpb = kw // gc
    n_kb = g // gpb
    dt = jnp.exp(log_dt)[:, None]
    ld_re, ld_im = a_re * dt, a_im * dt
    e = jnp.exp(ld_re)
    abar_re, abar_im = e * jnp.cos(ld_im), e * jnp.sin(ld_im)
    den = a_re * a_re + a_im * a_im
    q_re = ((abar_re - 1.0) * a_re + abar_im * a_im) / den
    q_im = (abar_im * a_re - (abar_re - 1.0) * a_im) / den
    bb_re = q_re[..., None] * b_re - q_im[..., None] * b_im
    bb_im = q_re[..., None] * b_im + q_im[..., None] * b_re
    eye = jnp.eye(gpb, dtype=F32)

    def pack_b(bb):
        bb = bb.reshape(n_kb, gpb, n, gc)
        return jnp.einsum('kgnc,gh->kgchn', bb, eye).reshape(n_kb, gpb * gc, gpb * n)

    def pack_c(cc):
        cc = cc.reshape(n_kb, gpb, gc, n)
        return jnp.einsum('kgcn,gh->kgnhc', cc, eye).reshape(n_kb, gpb * n, gpb * gc)

    wb = jnp.concatenate([pack_b(bb_re), pack_b(bb_im)], axis=2).astype(BF16)
    wc = jnp.concatenate([pack_c(c_re), pack_c(-c_im)], axis=1).astype(BF16)
    return abar_re.reshape(1, g * n), abar_im.reshape(1, g * n), wb, wc


def _pad_cols(w, n):
    return jnp.pad(w, ((0, 0), (0, n - w.shape[1])))


def _layer(x, cfg, cos, sin, st, p):
    seq, bsz, sq, past = cfg["seq"], cfg["bsz"], cfg["sq"], cfg["past"]
    t_all, d = x.shape
    n_s = bsz * sq
    width = cfg["mix_w"]
    n_heads, c_heads = cfg["a_heads"], cfg["c_heads"]

    ff = p["ffn_w_down"].shape[1]
    ffp = _round_up(ff, 1024)

    def ffn(x, idx, g_pre, g_post):
        w_up = jnp.concatenate([_pad_cols(p["ffn_w_up"][idx][:, :ff], ffp),
                                _pad_cols(p["ffn_w_up"][idx][:, ff:], ffp)], axis=1).astype(BF16)
        w_down = jnp.pad(p["ffn_w_down"][idx], ((0, ffp - ff), (0, 0))).astype(BF16)
        hid = swiglu_up(rmsnorm(x, g_pre), w_up, ffp)
        y = matmul(hid, w_down, out_dtype=F32, name="ffn_down")
        return add_rmsnorm(x, y, g_post, 0.5)

    g = p["norm_g"]
    x = ffn(x, 0, g[0], g[1])

    a_cols = cfg["q_lora"] + cfg["kv_lora"] + cfg["a_rope"]
    cpad = _round_up(cfg["q_lora"] + cfg["kv_lora"] + LANE, width)
    c_cols = 3 * c_heads * LANE
    o_b, o_c, o_d, o_g = a_cols, a_cols + width, a_cols + width + c_cols, a_cols + 2 * width + c_cols
    w_in = p["w_in"]
    w_segf = jnp.concatenate([_pad_cols(w_in[:, :a_cols], cpad), w_in[:, o_b:o_c], w_in[:, o_d:o_g]],
                             axis=1).astype(BF16)
    h2 = rmsnorm(x, g[2])
    segf = matmul(h2, w_segf, out_dtype=F32, name="w_in_f32")
    qkvc = matmul(h2, w_in[:, o_c:o_d].astype(BF16), out_dtype=F32, name="w_in_qkv")
    gates = matmul(h2, w_in[:, o_g:].astype(BF16), out_dtype=BF16, epilogue=_ep_sigmoid, name="w_in_gates")
    ub_col = cpad // width
    ud_col = ub_col + 1

    nope = LANE
    w_uq = p["mla_w_uq"].reshape(cfg["q_lora"], n_heads, nope + cfg["a_rope"])
    w_uq = jnp.pad(w_uq, ((0, 0), (0, 0), (0, 2 * LANE - nope - cfg["a_rope"])))
    w_uq = w_uq.reshape(cfg["q_lora"], n_heads * 2 * LANE).astype(BF16)
    w_ukv = p["mla_w_ukv"].astype(BF16)
    cq, ckv, ckv_b, kr, kr_b = mla_prep(segf, p["mla_q_norm"], p["mla_kv_norm"], cos, sin, cpad)
    tm_q = _tile(t_all, 1024)
    q = matmul(cq, w_uq, out_dtype=BF16, tm=tm_q, tn=1024, name="mla_q",
               epilogue=functools.partial(_ep_rope_q, scale=(nope + cfg["a_rope"]) ** -0.5, half=cfg["a_rope"] // 2),
               extra=[(cos, (tm_q, LANE), lambda i, j: (i, 0)), (sin, (tm_q, LANE), lambda i, j: (i, 0))])
    kv_p = matmul(ckv_b, w_ukv, out_dtype=BF16, m=seq, name="mla_kv_p")
    oa_p = mla_flash(q, kv_p, kr_b, seq, n_heads)
    n_keys = past + sq
    kp = _round_up(n_keys, LANE)
    ckv_all = jnp.concatenate([st["ckv"].astype(BF16), ckv_b[seq:].reshape(bsz, sq, -1),
                               jnp.zeros((bsz, kp - n_keys, cfg["kv_lora"]), BF16)], axis=1)
    kr_all = jnp.concatenate([_pad_cols(st["kr"].reshape(bsz * past, -1), LANE).reshape(bsz, past, LANE).astype(BF16),
                              kr_b[seq:].reshape(bsz, sq, LANE), jnp.zeros((bsz, kp - n_keys, LANE), BF16)], axis=1)
    kv_s = matmul(ckv_all.reshape(bsz * kp, -1), w_ukv, out_dtype=BF16, name="mla_kv_s")
    oa_s = mla_sample(q, kv_s, kr_all.reshape(bsz * kp, LANE), seq, bsz, sq, n_heads, n_keys, past)
    o_a = jnp.concatenate([oa_p, oa_s], axis=0)

    lru = (p["lru_conv_w"], p["lru_conv_b"], p["lru_w_a"].astype(BF16), p["lru_b_a"],
           p["lru_w_x"].astype(BF16), p["lru_b_x"], p["lru_lambda"])
    n_conv = p["lru_conv_w"].shape[0] - 1
    ob_p, hl_p = rglru(segf, 0, ub_col, 1, seq, jnp.zeros((1, width), F32), jnp.zeros((1, n_conv, width), F32), *lru)
    ob_s, hl_s = rglru(segf, seq, ub_col, bsz, sq, st["h"], st["conv"], *lru)
    o_bb = jnp.concatenate([ob_p, ob_s], axis=0)
    u_b = segf[:, cpad:cpad + width]
    conv_p = u_b[seq - n_conv:seq][None]
    conv_s = u_b[seq:].reshape(bsz, sq, width)[:, sq - n_conv:]

    win = BAND_CHUNKS * CHUNK
    tq_c = win
    assert seq % tq_c == 0
    bias_pp, bias_po = band_bias_tables(p["band_rel_bias"], tq_c, win, tq_c, tq_c)
    kcol, vcol = c_heads, 2 * c_heads
    prev_map = lambda col: (lambda i, h: (jnp.maximum(i - 1, 0), col + h))
    own_map = lambda col: (lambda i, h: (i, col + h))
    oc_p = band_attention(qkvc, 0, 0, qkvc, prev_map(kcol), qkvc, prev_map(vcol), qkvc, own_map(kcol),
                          qkvc, own_map(vcol), bias_pp, bias_po, n_heads=c_heads, n_blocks=seq // tq_c,
                          tq=tq_c, prev=win, own=tq_c, mask_first_prev=True)
    own_s = LANE
    k_new = qkvc[seq:, c_heads * LANE:2 * c_heads * LANE].reshape(bsz, sq, c_heads * LANE)
    v_new = qkvc[seq:, 2 * c_heads * LANE:].reshape(bsz, sq, c_heads * LANE)
    pad_own = lambda a: jnp.pad(a, ((0, 0), (0, own_s - sq), (0, 0))).reshape(bsz * own_s, c_heads * LANE)
    bias_sp, bias_so = band_bias_tables(p["band_rel_bias"], sq, win, own_s, sq)
    cache_k = st["bk"].reshape(bsz * win, c_heads * LANE)
    cache_v = st["bv"].reshape(bsz * win, c_heads * LANE)
    batch_map = lambda i, h: (i, h)
    oc_s = band_attention(qkvc, seq, 0, cache_k, batch_map, cache_v, batch_map, pad_own(k_new), batch_map,
                          pad_own(v_new), batch_map, bias_sp, bias_so, n_heads=c_heads, n_blocks=bsz,
                          tq=sq, prev=win, own=own_s, mask_first_prev=False)
    o_cc = jnp.concatenate([oc_p, oc_s], axis=0)
    keep = min(win, seq)
    k_all = qkvc[:, c_heads * LANE:2 * c_heads * LANE]
    v_all = qkvc[:, 2 * c_heads * LANE:]
    bk_p = k_all[seq - keep:seq].reshape(1, keep, c_heads, LANE)
    bv_p = v_all[seq - keep:seq].reshape(1, keep, c_heads, LANE)
    bk_s = k_new.reshape(bsz, sq, c_heads, LANE)
    bv_s = v_new.reshape(bsz, sq, c_heads, LANE)

    n_grp, n_state = p["s5_a_re"].shape
    abar_re, abar_im, wb, wc = s5_params(p["s5_a_re"], p["s5_a_im"], p["s5_log_dt"], p["s5_b_re"], p["s5_b_im"],
                                         p["s5_c_re"], p["s5_c_im"], LANE)
    pack0 = lambda re, im: jnp.concatenate([re.reshape(-1, 1, n_grp * n_state), im.reshape(-1, 1, n_grp * n_state)],
                                           axis=2)
    zeros0 = jnp.zeros((1, n_grp, n_state), F32)
    z_p, xl_p = s5_scan(segf, 0, ud_col, 1, seq, pack0(zeros0, zeros0), abar_re, abar_im, wb, wc, p["s5_d"])
    z_s, xl_s = s5_scan(segf, seq, ud_col, bsz, sq, pack0(st["s5re"], st["s5im"]), abar_re, abar_im, wb, wc,
                        p["s5_d"])
    z = jnp.concatenate([z_p, z_s], axis=0)
    tm_g = _tile(t_all, 1024)
    tn_g = _tile(width, 1024)
    o_dd = matmul(z.astype(BF16), p["s5_w_glu"].astype(BF16), out_dtype=BF16, tm=tm_g, tn=tn_g, epilogue=_ep_glu,
                  extra=[(z, (tm_g, tn_g), lambda i, j: (i, j)),
                         (p["s5_b_glu"].reshape(1, -1), (1, tn_g), lambda i, j: (0, j))], name="s5_glu")
    unpack = lambda xl, part: xl[:, part * n_grp * n_state:(part + 1) * n_grp * n_state].reshape(-1, n_grp, n_state)

    merged = merge_branches((o_a, o_bb, o_cc, o_dd), p["w_branch"].astype(BF16), gates)
    y = matmul(merged, p["w_out"].astype(BF16), out_dtype=F32, name="w_out")
    x = add_rmsnorm(x, y, g[3], 1.0)
    x = ffn(x, 1, g[4], g[5])

    new_p = (ckv[:seq][None], kr[:seq, :cfg["a_rope"]][None], hl_p, conv_p, bk_p, bv_p, unpack(xl_p, 0),
             unpack(xl_p, 1))
    new_s = (ckv[seq:].reshape(bsz, sq, -1), kr[seq:, :cfg["a_rope"]].reshape(bsz, sq, -1), hl_s, conv_s, bk_s, bv_s,
             unpack(xl_s, 0), unpack(xl_s, 1))
    return x, new_p, new_s


def kernel(x_prompt, x_sample, cache_mla_ckv, cache_mla_krope, state_lru_h, state_lru_conv, cache_band_k, cache_band_v, state_s5_re, state_s5_im, norm_g, ffn_w_up, ffn_w_down, w_in, mla_q_norm, mla_kv_norm, mla_w_uq, mla_w_ukv, lru_conv_w, lru_conv_b, lru_w_a, lru_b_a, lru_w_x, lru_b_x, lru_lambda, band_rel_bias, s5_a_re, s5_a_im, s5_log_dt, s5_b_re, s5_b_im, s5_c_re, s5_c_im, s5_d, s5_w_glu, s5_b_glu, w_branch, w_out):
    depth = norm_g.shape[0]
    n_p, seq, d = x_prompt.shape
    bsz, sq, _ = x_sample.shape
    past = cache_mla_ckv.shape[2]
    a_rope = cache_mla_krope.shape[3]
    mix_w = w_branch.shape[2]
    a_heads = mix_w // LANE
    c_heads = cache_band_k.shape[3]
    assert n_p == 1 and a_rope == 2 * ROPE_HALF and cache_band_k.shape[4] == LANE
    assert mla_w_uq.shape[2] == a_heads * (LANE + a_rope) and mla_w_ukv.shape[2] == a_heads * 2 * LANE
    assert past % CHUNK == 0 and cache_band_k.shape[2] == BAND_CHUNKS * CHUNK and past >= BAND_CHUNKS * CHUNK
    assert sq <= CHUNK and seq % (BAND_CHUNKS * CHUNK) == 0 and sq >= lru_conv_w.shape[1] - 1
    cfg = dict(seq=seq, bsz=bsz, sq=sq, past=past, mix_w=mix_w, a_heads=a_heads, c_heads=c_heads,
               q_lora=mla_q_norm.shape[1], kv_lora=mla_kv_norm.shape[1], a_rope=a_rope)

    pos = np.concatenate([np.arange(seq), np.tile(past + np.arange(sq), bsz)]).astype(np.float32)
    half = a_rope // 2
    inv = ROPE_THETA ** (-jnp.arange(half, dtype=F32) / half)
    ang = jnp.asarray(pos)[:, None] * inv
    zpad = jnp.zeros((pos.shape[0], LANE - a_rope), F32)
    cos = jnp.concatenate([jnp.cos(ang), jnp.cos(ang), zpad], axis=1)
    sin = jnp.concatenate([-jnp.sin(ang), jnp.sin(ang), zpad], axis=1)

    x = jnp.concatenate([x_prompt.reshape(seq, d), x_sample.reshape(bsz * sq, d)], axis=0)
    new_p, new_s = [], []
    for l in range(depth):
        st = dict(ckv=cache_mla_ckv[l], kr=cache_mla_krope[l], h=state_lru_h[l], conv=state_lru_conv[l],
                  bk=cache_band_k[l], bv=cache_band_v[l], s5re=state_s5_re[l], s5im=state_s5_im[l])
        p = dict(norm_g=norm_g[l], ffn_w_up=ffn_w_up[l], ffn_w_down=ffn_w_down[l], w_in=w_in[l],
                 mla_q_norm=mla_q_norm[l], mla_kv_norm=mla_kv_norm[l], mla_w_uq=mla_w_uq[l], mla_w_ukv=mla_w_ukv[l],
                 lru_conv_w=lru_conv_w[l], lru_conv_b=lru_conv_b[l], lru_w_a=lru_w_a[l], lru_b_a=lru_b_a[l],
                 lru_w_x=lru_w_x[l], lru_b_x=lru_b_x[l], lru_lambda=lru_lambda[l], band_rel_bias=band_rel_bias[l],
                 s5_a_re=s5_a_re[l], s5_a_im=s5_a_im[l], s5_log_dt=s5_log_dt[l], s5_b_re=s5_b_re[l],
                 s5_b_im=s5_b_im[l], s5_c_re=s5_c_re[l], s5_c_im=s5_c_im[l], s5_d=s5_d[l], s5_w_glu=s5_w_glu[l],
                 s5_b_glu=s5_b_glu[l], w_branch=w_branch[l], w_out=w_out[l])
        x, st_p, st_s = _layer(x, cfg, cos, sin, st, p)
        new_p.append(st_p)
        new_s.append(st_s)

    outs = [x[:seq].reshape(1, seq, d), x[seq:].reshape(bsz, sq, d)]
    for i in range(8):
        outs.append(jnp.stack([s[i] for s in new_p], axis=0))
        outs.append(jnp.stack([s[i] for s in new_s], axis=0))
    return tuple(outs)
```

```python
import functools
import math

import jax
import jax.numpy as jnp
import numpy as np
from jax import lax
from jax.experimental import pallas as pl
from jax.experimental.pallas import tpu as pltpu

F32 = jnp.float32
BF16 = jnp.bfloat16

CHUNK = 64
BAND_CHUNKS = 8
EPS = 1e-6
ROPE_THETA = 10000.0
LRU_C = 8.0
ROPE_HALF = 32
MASK_VALUE = -1e30
LANE = 128
SUBLANE = 8
VMEM_LIMIT = 56 * 1024 * 1024


def _cparams(*sem):
    return pltpu.CompilerParams(dimension_semantics=sem, vmem_limit_bytes=VMEM_LIMIT)


def _round_up(x, m):
    return (x + m - 1) // m * m


def _tile(dim, pref):
    if dim <= pref:
        return dim
    t = pref
    while dim % t:
        t //= 2
    assert t >= SUBLANE, (dim, pref)
    return t


def _rmsnorm_kernel(x_ref, g_ref, o_ref):
    x = x_ref[...]
    ms = jnp.mean(x * x, axis=-1, keepdims=True)
    o_ref[...] = (x * lax.rsqrt(ms + EPS) * g_ref[...]).astype(o_ref.dtype)


def rmsnorm(x, g, out_dtype=BF16):
    t, d = x.shape
    tm = _tile(t, 256)
    return pl.pallas_call(
        _rmsnorm_kernel,
        grid=(t // tm,),
        in_specs=[pl.BlockSpec((tm, d), lambda i: (i, 0)), pl.BlockSpec((1, d), lambda i: (0, 0))],
        out_specs=pl.BlockSpec((tm, d), lambda i: (i, 0)),
        out_shape=jax.ShapeDtypeStruct((t, d), out_dtype),
        compiler_params=_cparams("parallel"),
        name="rmsnorm",
    )(x, g.reshape(1, d))


def _add_rmsnorm_kernel(x_ref, y_ref, g_ref, o_ref, *, scale):
    y = y_ref[...]
    ms = jnp.mean(y * y, axis=-1, keepdims=True)
    o_ref[...] = x_ref[...] + scale * (y * lax.rsqrt(ms + EPS) * g_ref[...])


def add_rmsnorm(x, y, g, scale):
    t, d = x.shape
    tm = _tile(t, 256)
    return pl.pallas_call(
        functools.partial(_add_rmsnorm_kernel, scale=scale),
        grid=(t // tm,),
        in_specs=[pl.BlockSpec((tm, d), lambda i: (i, 0)), pl.BlockSpec((tm, d), lambda i: (i, 0)),
                  pl.BlockSpec((1, d), lambda i: (0, 0))],
        out_specs=pl.BlockSpec((tm, d), lambda i: (i, 0)),
        out_shape=jax.ShapeDtypeStruct((t, d), F32),
        compiler_params=_cparams("parallel"),
        name="add_rmsnorm",
    )(x, y, g.reshape(1, d))


def _k_tile(k):
    if k <= 4096:
        return k
    for parts in range(2, k // LANE + 1):
        if k % parts == 0 and (k // parts) % LANE == 0 and k // parts <= 4096:
            return k // parts
    raise ValueError(k)


def _mm_kernel(*refs, nk, n_extra, epilogue):
    a_ref, b_ref = refs[0], refs[1]
    extra = refs[2:2 + n_extra]
    o_ref = refs[2 + n_extra]
    if nk == 1:
        acc = jnp.dot(a_ref[...], b_ref[...], preferred_element_type=F32)
        o_ref[...] = epilogue(acc, *extra).astype(o_ref.dtype)
        return
    acc_ref = refs[3 + n_extra]
    k = pl.program_id(2)

    @pl.when(k == 0)
    def _():
        acc_ref[...] = jnp.zeros_like(acc_ref)

    acc_ref[...] += jnp.dot(a_ref[...], b_ref[...], preferred_element_type=F32)

    @pl.when(k == nk - 1)
    def _():
        o_ref[...] = epilogue(acc_ref[...], *extra).astype(o_ref.dtype)


def _ep_none(acc):
    return acc


def _ep_sigmoid(acc):
    return jax.nn.sigmoid(acc)


def _ep_glu(acc, z_ref, b_ref):
    z = z_ref[...]
    return z * jax.nn.sigmoid(acc + b_ref[...])


def _ep_rope_q(acc, cos_ref, sin_ref, *, scale, half):
    cos, sin = cos_ref[...], sin_ref[...]
    lane = lax.broadcasted_iota(jnp.int32, cos.shape, 1)
    outs = []
    for h in range(acc.shape[1] // (2 * LANE)):
        nope = acc[:, h * 2 * LANE:h * 2 * LANE + LANE]
        r = acc[:, h * 2 * LANE + LANE:(h + 1) * 2 * LANE]
        swapped = jnp.where(lane < half, pltpu.roll(r, LANE - half, axis=1), pltpu.roll(r, half, axis=1))
        outs += [nope * scale, (r * cos + swapped * sin) * scale]
    return jnp.concatenate(outs, axis=1)


def matmul(a, b, *, out_dtype, m=None, tm=1024, tn=1024, epilogue=_ep_none, extra=(), name="matmul"):
    m = a.shape[0] if m is None else m
    k, n = b.shape
    assert a.shape[1] == k
    tm, tn, tk = _tile(m, tm), _tile(n, tn), _k_tile(k)
    nk = k // tk
    if nk == 1:
        grid = (m // tm, n // tn)
        wrap = lambda f: f
        a_spec = pl.BlockSpec((tm, k), lambda i, j: (i, 0))
        b_spec = pl.BlockSpec((k, tn), lambda i, j: (0, j))
        o_spec = pl.BlockSpec((tm, tn), lambda i, j: (i, j))
        scratch = []
        sem = ("parallel", "parallel")
    else:
        grid = (m // tm, n // tn, nk)
        wrap = lambda f: (lambda i, j, kk: f(i, j))
        a_spec = pl.BlockSpec((tm, tk), lambda i, j, kk: (i, kk))
        b_spec = pl.BlockSpec((tk, tn), lambda i, j, kk: (kk, j))
        o_spec = pl.BlockSpec((tm, tn), lambda i, j, kk: (i, j))
        scratch = [pltpu.VMEM((tm, tn), F32)]
        sem = ("parallel", "parallel", "arbitrary")
    extra_specs = [pl.BlockSpec(bs, wrap(im)) for _, bs, im in extra]
    return pl.pallas_call(
        functools.partial(_mm_kernel, nk=nk, n_extra=len(extra), epilogue=epilogue),
        grid=grid,
        in_specs=[a_spec, b_spec] + extra_specs,
        out_specs=o_spec,
        out_shape=jax.ShapeDtypeStruct((m, n), out_dtype),
        scratch_shapes=scratch,
        compiler_params=_cparams(*sem),
        name=name,
    )(a, b, *[e[0] for e in extra])


def _swiglu_kernel(a_ref, wg_ref, wu_ref, o_ref):
    a = a_ref[...]
    gate = jnp.dot(a, wg_ref[...], preferred_element_type=F32)
    up = jnp.dot(a, wu_ref[...], preferred_element_type=F32)
    o_ref[...] = (jax.nn.silu(gate) * up).astype(o_ref.dtype)


def swiglu_up(a, w_up, ffp):
    t, d = a.shape
    tm, tn = _tile(t, 1024), _tile(ffp, 512)
    nj = ffp // tn
    return pl.pallas_call(
        _swiglu_kernel,
        grid=(t // tm, nj),
        in_specs=[pl.BlockSpec((tm, d), lambda i, j: (i, 0)),
                  pl.BlockSpec((d, tn), lambda i, j: (0, j)),
                  pl.BlockSpec((d, tn), lambda i, j: (0, j + nj))],
        out_specs=pl.BlockSpec((tm, tn), lambda i, j: (i, j)),
        out_shape=jax.ShapeDtypeStruct((t, ffp), BF16),
        compiler_params=_cparams("parallel", "parallel"),
        name="swiglu_up",
    )(a, w_up, w_up)


def _merge_kernel(oa_ref, ob_ref, oc_ref, od_ref, wb_ref, ga_ref, gb_ref, gc_ref, gd_ref, o_ref):
    total = None
    for i, (o_r, g_r) in enumerate(((oa_ref, ga_ref), (ob_ref, gb_ref), (oc_ref, gc_ref), (od_ref, gd_ref))):
        part = g_r[...].astype(F32) * jnp.dot(o_r[...], wb_ref[i], preferred_element_type=F32)
        total = part if total is None else total + part
    o_ref[...] = total.astype(o_ref.dtype)


def merge_branches(outs, w_branch, gates):
    t, w = outs[0].shape
    d = w_branch.shape[2]
    tm, tn = _tile(t, 512), _tile(d, 512)
    nj = d // tn
    o_spec = pl.BlockSpec((tm, w), lambda i, j: (i, 0))
    g_specs = [pl.BlockSpec((tm, tn), functools.partial(lambda i, j, b: (i, j + b * nj), b=b)) for b in range(4)]
    return pl.pallas_call(
        _merge_kernel,
        grid=(t // tm, nj),
        in_specs=[o_spec] * 4 + [pl.BlockSpec((4, w, tn), lambda i, j: (0, 0, j))] + g_specs,
        out_specs=pl.BlockSpec((tm, tn), lambda i, j: (i, j)),
        out_shape=jax.ShapeDtypeStruct((t, d), BF16),
        compiler_params=_cparams("parallel", "parallel"),
        name="merge_branches",
    )(*outs, w_branch, gates, gates, gates, gates)


def _mla_prep_kernel(c_ref, qn_ref, kvn_ref, cos_ref, sin_ref, cq_ref, ckv_ref, ckvb_ref, kr_ref, krb_ref,
                     *, q_lora, kv_lora, half):
    c_q = c_ref[:, :q_lora]
    ms = jnp.mean(c_q * c_q, axis=-1, keepdims=True)
    cq_ref[...] = (c_q * lax.rsqrt(ms + EPS) * qn_ref[...]).astype(cq_ref.dtype)
    c_kv = c_ref[:, q_lora:q_lora + kv_lora]
    ms = jnp.mean(c_kv * c_kv, axis=-1, keepdims=True)
    ckv = c_kv * lax.rsqrt(ms + EPS) * kvn_ref[...]
    ckv_ref[...] = ckv
    ckvb_ref[...] = ckv.astype(ckvb_ref.dtype)
    r = c_ref[:, q_lora + kv_lora:q_lora + kv_lora + LANE]
    lane = lax.broadcasted_iota(jnp.int32, r.shape, 1)
    swapped = jnp.where(lane < half, pltpu.roll(r, LANE - half, axis=1), pltpu.roll(r, half, axis=1))
    kr = r * cos_ref[...] + swapped * sin_ref[...]
    kr_ref[...] = kr
    krb_ref[...] = kr.astype(krb_ref.dtype)


def mla_prep(segf, q_norm, kv_norm, cos, sin, cpad):
    t = segf.shape[0]
    q_lora, kv_lora = q_norm.shape[0], kv_norm.shape[0]
    tm = _tile(t, 256)
    row = lambda w: pl.BlockSpec((tm, w), lambda i: (i, 0))
    const = lambda w: pl.BlockSpec((1, w), lambda i: (0, 0))
    return pl.pallas_call(
        functools.partial(_mla_prep_kernel, q_lora=q_lora, kv_lora=kv_lora, half=ROPE_HALF),
        grid=(t // tm,),
        in_specs=[row(cpad), const(q_lora), const(kv_lora), row(LANE), row(LANE)],
        out_specs=[row(q_lora), row(kv_lora), row(kv_lora), row(LANE), row(LANE)],
        out_shape=[jax.ShapeDtypeStruct((t, q_lora), BF16), jax.ShapeDtypeStruct((t, kv_lora), F32),
                   jax.ShapeDtypeStruct((t, kv_lora), BF16), jax.ShapeDtypeStruct((t, LANE), F32),
                   jax.ShapeDtypeStruct((t, LANE), BF16)],
        compiler_params=_cparams("parallel"),
        name="mla_prep",
    )(segf, q_norm.reshape(1, -1), kv_norm.reshape(1, -1), cos, sin)


def _mla_flash_kernel(qi_ref, ki_ref, q_ref, kv_ref, kr_ref, o_ref, m_ref, acc_ref, *, heads, tq, tk):
    s_id = pl.program_id(1)
    qi, ki = qi_ref[s_id], ki_ref[s_id]
    hw = 2 * LANE

    @pl.when(ki == 0)
    def _():
        m_ref[...] = jnp.full_like(m_ref, -jnp.inf)
        acc_ref[...] = jnp.zeros_like(acc_ref)

    def step(masked):
        kr = kr_ref[...]
        ones = jnp.ones((tk, LANE), BF16)
        if masked:
            rc = lax.broadcasted_iota(jnp.int32, (tq, tk), 0) // CHUNK
            cc = lax.broadcasted_iota(jnp.int32, (tq, tk), 1) // CHUNK
            visible = cc <= rc
        for h in range(heads):
            q = q_ref[:, h * hw:(h + 1) * hw]
            k = jnp.concatenate([kv_ref[:, h * hw:h * hw + LANE], kr], axis=1)
            v1 = jnp.concatenate([kv_ref[:, h * hw + LANE:(h + 1) * hw], ones], axis=1)
            s = lax.dot_general(q, k, (((1,), (1,)), ((), ())), preferred_element_type=F32)
            if masked:
                s = jnp.where(visible, s, MASK_VALUE)
            m_prev = m_ref[h]
            m_next = jnp.maximum(m_prev, jnp.max(s, axis=1)[:, None])
            p = jnp.exp2(s - jnp.tile(m_next, (1, tk // LANE)))
            alpha = jnp.exp2(m_prev - m_next)
            m_ref[h] = m_next
            pv = jnp.dot(p.astype(BF16), v1, preferred_element_type=F32)
            acc_ref[:, h * hw:(h + 1) * hw] = jnp.tile(alpha, (1, 2)) * acc_ref[:, h * hw:(h + 1) * hw] + pv

    @pl.when(ki < qi)
    def _():
        step(False)

    @pl.when(ki == qi)
    def _():
        step(True)
        for h in range(heads):
            o_ref[:, h * LANE:(h + 1) * LANE] = (
                acc_ref[:, h * hw:h * hw + LANE] / acc_ref[:, h * hw + LANE:(h + 1) * hw]).astype(o_ref.dtype)


def mla_flash(q, kv, kr, seq, n_heads, *, tile=1024, heads_per_step=2):
    tq = tk = _tile(seq, tile)
    hg = min(heads_per_step, n_heads)
    assert n_heads % hg == 0 and tq % CHUNK == 0
    nq = seq // tq
    pairs = [(i, j) for i in range(nq) for j in range(i + 1)]
    qi_arr = jnp.asarray(np.array([p[0] for p in pairs], np.int32))
    ki_arr = jnp.asarray(np.array([p[1] for p in pairs], np.int32))
    grid_spec = pltpu.PrefetchScalarGridSpec(
        num_scalar_prefetch=2,
        grid=(n_heads // hg, len(pairs)),
        in_specs=[pl.BlockSpec((tq, hg * 2 * LANE), lambda g, s, qi, ki: (qi[s], g)),
                  pl.BlockSpec((tk, hg * 2 * LANE), lambda g, s, qi, ki: (ki[s], g)),
                  pl.BlockSpec((tk, LANE), lambda g, s, qi, ki: (ki[s], 0))],
        out_specs=pl.BlockSpec((tq, hg * LANE), lambda g, s, qi, ki: (qi[s], g)),
        scratch_shapes=[pltpu.VMEM((hg, tq, LANE), F32), pltpu.VMEM((tq, hg * 2 * LANE), F32)],
    )
    return pl.pallas_call(
        functools.partial(_mla_flash_kernel, heads=hg, tq=tq, tk=tk),
        grid_spec=grid_spec,
        out_shape=jax.ShapeDtypeStruct((seq, n_heads * LANE), BF16),
        compiler_params=_cparams("parallel", "arbitrary"),
        name="mla_flash",
    )(qi_arr, ki_arr, q, kv, kr)


def _mla_sample_kernel(q_ref, kv_ref, kr_ref, o_ref, *, heads, n_keys, past):
    sq, kp = q_ref.shape[0], kv_ref.shape[0]
    kr = kr_ref[...]
    qpos = past + lax.broadcasted_iota(jnp.int32, (sq, kp), 0)
    kpos = lax.broadcasted_iota(jnp.int32, (sq, kp), 1)
    visible = (kpos < n_keys) & (kpos // CHUNK <= qpos // CHUNK)
    for h in range(heads):
        q = q_ref[:, h * 2 * LANE:(h + 1) * 2 * LANE]
        k = jnp.concatenate([kv_ref[:, h * 2 * LANE:h * 2 * LANE + LANE], kr], axis=1)
        v = kv_ref[:, h * 2 * LANE + LANE:(h + 1) * 2 * LANE]
        s = lax.dot_general(q, k, (((1,), (1,)), ((), ())), preferred_element_type=F32)
        s = jnp.where(visible, s, MASK_VALUE)
        p = jnp.exp2(s - jnp.max(s, axis=-1, keepdims=True))
        denom = jnp.sum(p, axis=-1, keepdims=True)
        pv = jnp.dot(p.astype(v.dtype), v, preferred_element_type=F32)
        o_ref[:, h * LANE:(h + 1) * LANE] = (pv / denom).astype(o_ref.dtype)


def mla_sample(q, kv, kr, row0, bsz, sq, n_heads, n_keys, past):
    kp = kv.shape[0] // bsz
    assert row0 % sq == 0
    return pl.pallas_call(
        functools.partial(_mla_sample_kernel, heads=n_heads, n_keys=n_keys, past=past),
        grid=(bsz,),
        in_specs=[pl.BlockSpec((sq, n_heads * 2 * LANE), lambda b: (row0 // sq + b, 0)),
                  pl.BlockSpec((kp, n_heads * 2 * LANE), lambda b: (b, 0)),
                  pl.BlockSpec((kp, LANE), lambda b: (b, 0))],
        out_specs=pl.BlockSpec((sq, n_heads * LANE), lambda b: (b, 0)),
        out_shape=jax.ShapeDtypeStruct((bsz * sq, n_heads * LANE), BF16),
        compiler_params=_cparams("parallel"),
        name="mla_sample",
    )(q, kv, kr)


def _band_kernel(q_ref, kp_ref, ko_ref, vp_ref, vo_ref, bp_ref, bo_ref, o_ref, *, scale, mask_first_prev):
    q = (q_ref[...] * scale).astype(BF16)
    dn = (((1,), (1,)), ((), ()))
    sp = lax.dot_general(q, kp_ref[...].astype(BF16), dn, preferred_element_type=F32) + bp_ref[0]
    so = lax.dot_general(q, ko_ref[...].astype(BF16), dn, preferred_element_type=F32) + bo_ref[0]
    if mask_first_prev:
        sp = jnp.where(pl.program_id(1) > 0, sp, MASK_VALUE)
    m = jnp.maximum(jnp.max(sp, axis=-1, keepdims=True), jnp.max(so, axis=-1, keepdims=True))
    pp, po = jnp.exp(sp - m), jnp.exp(so - m)
    denom = jnp.sum(pp, axis=-1, keepdims=True) + jnp.sum(po, axis=-1, keepdims=True)
    pv = (jnp.dot(pp.astype(BF16), vp_ref[...].astype(BF16), preferred_element_type=F32)
          + jnp.dot(po.astype(BF16), vo_ref[...].astype(BF16), preferred_element_type=F32))
    o_ref[...] = (pv / denom).astype(o_ref.dtype)


def band_attention(q_arr, q_row0, q_col0, kp_arr, kp_map, vp_arr, vp_map, ko_arr, ko_map, vo_arr, vo_map,
                   bias_p, bias_o, *, n_heads, n_blocks, tq, prev, own, mask_first_prev):
    hd = LANE
    assert q_row0 % tq == 0
    return pl.pallas_call(
        functools.partial(_band_kernel, scale=hd ** -0.5, mask_first_prev=mask_first_prev),
        grid=(n_heads, n_blocks),
        in_specs=[pl.BlockSpec((tq, hd), lambda h, i: (q_row0 // tq + i, q_col0 + h)),
                  pl.BlockSpec((prev, hd), lambda h, i: kp_map(i, h)),
                  pl.BlockSpec((own, hd), lambda h, i: ko_map(i, h)),
                  pl.BlockSpec((prev, hd), lambda h, i: vp_map(i, h)),
                  pl.BlockSpec((own, hd), lambda h, i: vo_map(i, h)),
                  pl.BlockSpec((1, tq, prev), lambda h, i: (h, 0, 0)),
                  pl.BlockSpec((1, tq, own), lambda h, i: (h, 0, 0))],
        out_specs=pl.BlockSpec((tq, hd), lambda h, i: (i, h)),
        out_shape=jax.ShapeDtypeStruct((n_blocks * tq, n_heads * hd), BF16),
        compiler_params=_cparams("parallel", "arbitrary"),
        name="band_attention",
    )(q_arr, kp_arr, ko_arr, vp_arr, vo_arr, bias_p, bias_o)


def band_bias_tables(rel_bias, tq, prev, own, own_valid):
    n_heads = rel_bias.shape[0]
    clip = (rel_bias.shape[1] - 1) // 2
    qr = np.arange(tq)[:, None]

    def table(krel, valid):
        nk, k0 = krel.shape[1], int(krel[0, 0])
        period = nk + tq
        e = np.zeros(period, np.int64)
        e[:nk] = -np.arange(nk)
        e[nk + 1:] = np.arange(tq - 1, 0, -1)
        per_offset = jnp.take(rel_bias, jnp.asarray(np.clip(e - k0, -clip, clip) + clip), axis=1)
        toeplitz = jnp.tile(per_offset, (1, tq))[:, :tq * (period - 1)].reshape(n_heads, tq, period - 1)[:, :, :nk]
        dc = qr // CHUNK - np.floor_divide(krel, CHUNK)
        vis = (dc >= 0) & (dc <= BAND_CHUNKS) & valid
        return jnp.where(jnp.asarray(vis)[None], toeplitz, MASK_VALUE)

    kp = np.arange(-prev, 0)[None, :]
    ko = np.arange(own)[None, :]
    return table(kp, np.ones_like(kp, bool)), table(ko, ko < own_valid)


def _rglru_kernel(u_ref, h0_ref, c0_ref, cw_ref, cb_ref, wa_ref, ba_ref, wx_ref, bx_ref, lam_ref,
                  o_ref, hl_ref, prev_ref, h_ref, a_ref, b_ref, *, n_blocks, bw):
    t = pl.program_id(1)
    tb, width = u_ref.shape

    @pl.when(t == 0)
    def _():
        prev_ref[...] = c0_ref[0]
        h_ref[...] = h0_ref[0]

    u = u_ref[...]
    cw = cw_ref[...]
    nw = 4
    xc = cb_ref[...] + cw[nw - 1:nw, :] * u
    for s in range(1, nw):
        xc = xc + cw[nw - 1 - s:nw - s, :] * pltpu.roll(u, s, axis=0)
    head = jnp.concatenate([prev_ref[...], u[:SUBLANE, :]], axis=0)
    xh = cb_ref[...] + cw[nw - 1:nw, :] * u[:SUBLANE, :]
    for s in range(1, nw):
        xh = xh + cw[nw - 1 - s:nw - s, :] * pltpu.roll(head, s, axis=0)[SUBLANE:, :]
    a_ref[:SUBLANE, :] = xh
    a_ref[SUBLANE:, :] = xc[SUBLANE:, :]
    xc = a_ref[...]
    prev_ref[...] = u[tb - SUBLANE:, :]

    neg_c_sp = -LRU_C * jax.nn.softplus(-lam_ref[...])
    for n in range(n_blocks):
        sl = slice(n * bw, (n + 1) * bw)
        xb = xc[:, sl].astype(BF16)
        r = jax.nn.sigmoid(jnp.dot(xb, wa_ref[n], preferred_element_type=F32) + ba_ref[:, sl])
        i = jax.nn.sigmoid(jnp.dot(xb, wx_ref[n], preferred_element_type=F32) + bx_ref[:, sl])
        log_a = neg_c_sp[:, sl] * r
        a = jnp.exp(log_a)
        a_ref[:, sl] = a
        b_ref[:, sl] = jnp.sqrt(-jnp.tanh(log_a) * (a * a + 1.0)) * (i * xc[:, sl])

    def row(j, h):
        h = a_ref[pl.ds(j, 1), :] * h + b_ref[pl.ds(j, 1), :]
        b_ref[pl.ds(j, 1), :] = h
        return h

    h = lax.fori_loop(0, tb, row, h_ref[...])
    h_ref[...] = h
    o_ref[...] = b_ref[...].astype(o_ref.dtype)
    hl_ref[0] = h


def rglru(u_arr, row0, col0, n_seq, seq, h0, conv0, conv_w, conv_b, w_a, b_a, w_x, b_x, lam):
    width = h0.shape[-1]
    n_blocks, bw = w_a.shape[0], w_a.shape[1]
    tb = _tile(seq, 256)
    nt = seq // tb
    assert row0 % tb == 0 and tb >= 2 * SUBLANE and conv_w.shape[0] == 4
    conv0p = jnp.concatenate([jnp.zeros((n_seq, SUBLANE - conv0.shape[1], width), F32), conv0], axis=1)
    cwp = jnp.concatenate([conv_w, jnp.zeros((SUBLANE - conv_w.shape[0], width), F32)], axis=0)
    vec = lambda: pl.BlockSpec((1, width), lambda s, t: (0, 0))
    wspec = lambda: pl.BlockSpec((n_blocks, bw, bw), lambda s, t: (0, 0, 0))
    out, hl = pl.pallas_call(
        functools.partial(_rglru_kernel, n_blocks=n_blocks, bw=bw),
        grid=(n_seq, nt),
        in_specs=[pl.BlockSpec((tb, width), lambda s, t: (row0 // tb + s * nt + t, col0)),
                  pl.BlockSpec((1, 1, width), lambda s, t: (s, 0, 0)),
                  pl.BlockSpec((1, SUBLANE, width), lambda s, t: (s, 0, 0)),
                  pl.BlockSpec((SUBLANE, width), lambda s, t: (0, 0)),
                  vec(), wspec(), vec(), wspec(), vec(), vec()],
        out_specs=[pl.BlockSpec((tb, width), lambda s, t: (s * nt + t, 0)),
                   pl.BlockSpec((1, 1, width), lambda s, t: (s, 0, 0))],
        out_shape=[jax.ShapeDtypeStruct((n_seq * seq, width), BF16),
                   jax.ShapeDtypeStruct((n_seq, 1, width), F32)],
        scratch_shapes=[pltpu.VMEM((SUBLANE, width), F32), pltpu.VMEM((1, width), F32),
                        pltpu.VMEM((tb, width), F32), pltpu.VMEM((tb, width), F32)],
        compiler_params=_cparams("parallel", "arbitrary"),
        name="rglru",
    )(u_arr, h0.reshape(n_seq, 1, width), conv0p, cwp, conv_b.reshape(1, -1), w_a, b_a.reshape(1, -1),
      w_x, b_x.reshape(1, -1), lam.reshape(1, -1))
    return out, hl.reshape(n_seq, width)


def _permute_rows(perm, x):
    hi = x.astype(BF16)
    rest = x - hi.astype(F32)
    mid = rest.astype(BF16)
    lo = (rest - mid.astype(F32)).astype(BF16)
    dot = lambda piece: jnp.dot(perm, piece, preferred_element_type=F32)
    return (dot(hi) + dot(mid)) + dot(lo)


def _s5_kernel(u_ref, x0_ref, pre_ref, pim_ref, wb_ref, wc_ref, d_ref, z_ref, xl_ref, up_ref, xs_ref, xc_ref,
               *, n_kb, strip):
    t = pl.program_id(1)
    tb, width = u_ref.shape
    seg = tb // SUBLANE
    half = xs_ref.shape[1] // 2
    kw = width // n_kb
    sw = half // n_kb

    @pl.when(t == 0)
    def _():
        xc_ref[...] = x0_ref[0]

    idx0 = lax.broadcasted_iota(jnp.int32, (tb, tb), 0)
    idx1 = lax.broadcasted_iota(jnp.int32, (tb, tb), 1)
    to_segments = jnp.where(idx1 == (idx0 % SUBLANE) * seg + idx0 // SUBLANE, 1.0, 0.0).astype(BF16)
    to_time = jnp.where(idx0 == (idx1 % SUBLANE) * seg + idx1 // SUBLANE, 1.0, 0.0).astype(BF16)
    u = _permute_rows(to_segments, u_ref[...])
    ub = u.astype(BF16)
    for kb in range(n_kb):
        bu = jnp.dot(ub[:, kb * kw:(kb + 1) * kw], wb_ref[kb], preferred_element_type=F32)
        xs_ref[:, kb * sw:(kb + 1) * sw] = bu[:, :sw]
        xs_ref[:, half + kb * sw:half + (kb + 1) * sw] = bu[:, sw:]

    for c in range(half // strip):
        re = slice(c * strip, (c + 1) * strip)
        im = slice(half + c * strip, half + (c + 1) * strip)
        a_re = jnp.broadcast_to(pre_ref[0:1, re], (SUBLANE, strip))
        a_im = jnp.broadcast_to(pim_ref[0:1, re], (SUBLANE, strip))

        def local(j, carry):
            x_re, x_im = carry
            rows = pl.ds(pl.multiple_of(j * SUBLANE, SUBLANE), SUBLANE)
            n_re = a_re * x_re - a_im * x_im + xs_ref[rows, re]
            n_im = a_re * x_im + a_im * x_re + xs_ref[rows, im]
            xs_ref[rows, re] = n_re
            xs_ref[rows, im] = n_im
            return n_re, n_im

        zero = jnp.zeros((SUBLANE, strip), F32)
        e_re, e_im = lax.fori_loop(0, seg, local, (zero, zero))

        al_re, al_im = pre_ref[seg - 1:seg, re], pim_ref[seg - 1:seg, re]
        c_re, c_im = xc_ref[:, re], xc_ref[:, im]
        ins_re, ins_im = [], []
        for s in range(SUBLANE):
            ins_re.append(c_re)
            ins_im.append(c_im)
            c_re, c_im = (al_re * c_re - al_im * c_im + e_re[s:s + 1, :],
                          al_re * c_im + al_im * c_re + e_im[s:s + 1, :])
        xc_ref[:, re] = c_re
        xc_ref[:, im] = c_im
        in_re = jnp.concatenate(ins_re, axis=0)
        in_im = jnp.concatenate(ins_im, axis=0)

        def fix(j, carry):
            rows = pl.ds(pl.multiple_of(j * SUBLANE, SUBLANE), SUBLANE)
            w_re, w_im = pre_ref[pl.ds(j, 1), re], pim_ref[pl.ds(j, 1), re]
            xs_ref[rows, re] = xs_ref[rows, re] + (w_re * in_re - w_im * in_im)
            xs_ref[rows, im] = xs_ref[rows, im] + (w_re * in_im + w_im * in_re)
            return carry

        lax.fori_loop(0, seg, fix, 0)

    xl_ref[0] = xc_ref[...]
    for kb in range(n_kb):
        xb = jnp.concatenate([xs_ref[:, kb * sw:(kb + 1) * sw],
                              xs_ref[:, half + kb * sw:half + (kb + 1) * sw]], axis=1).astype(BF16)
        y = jnp.dot(xb, wc_ref[kb], preferred_element_type=F32)
        cs = slice(kb * kw, (kb + 1) * kw)
        up_ref[:, cs] = jax.nn.gelu(y + d_ref[:, cs] * u[:, cs])
    z_ref[...] = _permute_rows(to_time, up_ref[...])


def s5_scan(u_arr, row0, col0, n_seq, seq, x0, ld_re, ld_im, wb, wc, d_skip):
    n_kb, kw, two_sw = wb.shape
    width = n_kb * kw
    half = n_kb * two_sw // 2
    tb = _tile(seq, 128)
    nt = seq // tb
    seg = tb // SUBLANE
    assert row0 % tb == 0 and tb % SUBLANE == 0
    steps = jnp.arange(1, seg + 1, dtype=F32)[:, None]
    mag = jnp.exp(steps * ld_re)
    pow_re, pow_im = mag * jnp.cos(steps * ld_im), mag * jnp.sin(steps * ld_im)
    z, xl = pl.pallas_call(
        functools.partial(_s5_kernel, n_kb=n_kb, strip=min(512, half)),
        grid=(n_seq, nt),
        in_specs=[pl.BlockSpec((tb, width), lambda s, t: (row0 // tb + s * nt + t, col0)),
                  pl.BlockSpec((1, 1, 2 * half), lambda s, t: (s, 0, 0)),
                  pl.BlockSpec((seg, half), lambda s, t: (0, 0)),
                  pl.BlockSpec((seg, half), lambda s, t: (0, 0)),
                  pl.BlockSpec(wb.shape, lambda s, t: (0, 0, 0)),
                  pl.BlockSpec(wc.shape, lambda s, t: (0, 0, 0)),
                  pl.BlockSpec((1, width), lambda s, t: (0, 0))],
        out_specs=[pl.BlockSpec((tb, width), lambda s, t: (s * nt + t, 0)),
                   pl.BlockSpec((1, 1, 2 * half), lambda s, t: (s, 0, 0))],
        out_shape=[jax.ShapeDtypeStruct((n_seq * seq, width), F32),
                   jax.ShapeDtypeStruct((n_seq, 1, 2 * half), F32)],
        scratch_shapes=[pltpu.VMEM((tb, width), F32), pltpu.VMEM((tb, 2 * half), F32),
                        pltpu.VMEM((1, 2 * half), F32)],
        compiler_params=_cparams("parallel", "arbitrary"),
        name="s5_scan",
    )(u_arr, x0, pow_re, pow_im, wb, wc, d_skip.reshape(1, -1))
    return z, xl.reshape(n_seq, 2 * half)


def s5_params(a_re, a_im, log_dt, b_re, b_im, c_re, c_im, kw):
    g, n, gc = b_re.shape
    gpb = kw // gc
    n_kb = g // gpb
    dt = jnp.exp(log_dt)[:, None]
    ld_re, ld_im = a_re * dt, a_im * dt
    e = jnp.exp(ld_re)
    abar_re, abar_im = e * jnp.cos(ld_im), e * jnp.sin(ld_im)
    den = a_re * a_re + a_im * a_im
    q_re = ((abar_re - 1.0) * a_re + abar_im * a_im) / den
    q_im = (abar_im * a_re - (abar_re - 1.0) * a_im) / den
    bb_re = q_re[..., None] * b_re - q_im[..., None] * b_im
    bb_im = q_re[..., None] * b_im + q_im[..., None] * b_re
    eye = jnp.eye(gpb, dtype=F32)

    def pack_b(bb):
        bb = bb.reshape(n_kb, gpb, n, gc)
        return jnp.einsum('kgnc,gh->kgchn', bb, eye).reshape(n_kb, gpb * gc, gpb * n)

    def pack_c(cc):
        cc = cc.reshape(n_kb, gpb, gc, n)
        return jnp.einsum('kgcn,gh->kgnhc', cc, eye).reshape(n_kb, gpb * n, gpb * gc)

    wb = jnp.concatenate([pack_b(bb_re), pack_b(bb_im)], axis=2).astype(BF16)
    wc = jnp.concatenate([pack_c(c_re), pack_c(-c_im)], axis=1).astype(BF16)
    return ld_re.reshape(1, g * n), ld_im.reshape(1, g * n), wb, wc


def _pad_cols(w, n):
    return jnp.pad(w, ((0, 0), (0, n - w.shape[1])))


def _layer(x, cfg, cos, sin, st, p):
    seq, bsz, sq, past = cfg["seq"], cfg["bsz"], cfg["sq"], cfg["past"]
    t_all, d = x.shape
    n_s = bsz * sq
    width = cfg["mix_w"]
    n_heads, c_heads = cfg["a_heads"], cfg["c_heads"]

    ff = p["ffn_w_down"].shape[1]
    ffp = _round_up(ff, 1024)

    def ffn(x, idx, g_pre, g_post):
        w_up = jnp.concatenate([_pad_cols(p["ffn_w_up"][idx][:, :ff], ffp),
                                _pad_cols(p["ffn_w_up"][idx][:, ff:], ffp)], axis=1).astype(BF16)
        w_down = jnp.pad(p["ffn_w_down"][idx], ((0, ffp - ff), (0, 0))).astype(BF16)
        hid = swiglu_up(rmsnorm(x, g_pre), w_up, ffp)
        y = matmul(hid, w_down, out_dtype=F32, name="ffn_down")
        return add_rmsnorm(x, y, g_post, 0.5)

    g = p["norm_g"]
    x = ffn(x, 0, g[0], g[1])

    a_cols = cfg["q_lora"] + cfg["kv_lora"] + cfg["a_rope"]
    cpad = _round_up(cfg["q_lora"] + cfg["kv_lora"] + LANE, width)
    c_cols = 3 * c_heads * LANE
    o_b, o_c, o_d, o_g = a_cols, a_cols + width, a_cols + width + c_cols, a_cols + 2 * width + c_cols
    w_in = p["w_in"]
    w_segf = jnp.concatenate([_pad_cols(w_in[:, :a_cols], cpad), w_in[:, o_b:o_c], w_in[:, o_d:o_g]],
                             axis=1).astype(BF16)
    h2 = rmsnorm(x, g[2])
    segf = matmul(h2, w_segf, out_dtype=F32, name="w_in_f32")
    qkvc = matmul(h2, w_in[:, o_c:o_d].astype(BF16), out_dtype=F32, name="w_in_qkv")
    gates = matmul(h2, w_in[:, o_g:].astype(BF16), out_dtype=BF16, epilogue=_ep_sigmoid, name="w_in_gates")
    ub_col = cpad // width
    ud_col = ub_col + 1

    nope = LANE
    w_uq = p["mla_w_uq"].reshape(cfg["q_lora"], n_heads, nope + cfg["a_rope"])
    w_uq = jnp.pad(w_uq, ((0, 0), (0, 0), (0, 2 * LANE - nope - cfg["a_rope"])))
    w_uq = w_uq.reshape(cfg["q_lora"], n_heads * 2 * LANE).astype(BF16)
    w_ukv = p["mla_w_ukv"].astype(BF16)
    cq, ckv, ckv_b, kr, kr_b = mla_prep(segf, p["mla_q_norm"], p["mla_kv_norm"], cos, sin, cpad)
    tm_q = _tile(t_all, 1024)
    q = matmul(cq, w_uq, out_dtype=BF16, tm=tm_q, tn=1024, name="mla_q",
               epilogue=functools.partial(_ep_rope_q, scale=(nope + cfg["a_rope"]) ** -0.5 * math.log2(math.e),
                                          half=cfg["a_rope"] // 2),
               extra=[(cos, (tm_q, LANE), lambda i, j: (i, 0)), (sin, (tm_q, LANE), lambda i, j: (i, 0))])
    kv_p = matmul(ckv_b, w_ukv, out_dtype=BF16, m=seq, name="mla_kv_p")
    oa_p = mla_flash(q, kv_p, kr_b, seq, n_heads)
    n_keys = past + sq
    kp = _round_up(n_keys, LANE)
    ckv_all = jnp.concatenate([st["ckv"].astype(BF16), ckv_b[seq:].reshape(bsz, sq, -1),
                               jnp.zeros((bsz, kp - n_keys, cfg["kv_lora"]), BF16)], axis=1)
    kr_all = jnp.concatenate([_pad_cols(st["kr"].reshape(bsz * past, -1), LANE).reshape(bsz, past, LANE).astype(BF16),
                              kr_b[seq:].reshape(bsz, sq, LANE), jnp.zeros((bsz, kp - n_keys, LANE), BF16)], axis=1)
    kv_s = matmul(ckv_all.reshape(bsz * kp, -1), w_ukv, out_dtype=BF16, name="mla_kv_s")
    oa_s = mla_sample(q, kv_s, kr_all.reshape(bsz * kp, LANE), seq, bsz, sq, n_heads, n_keys, past)
    o_a = jnp.concatenate([oa_p, oa_s], axis=0)

    lru = (p["lru_conv_w"], p["lru_conv_b"], p["lru_w_a"].astype(BF16), p["lru_b_a"],
           p["lru_w_x"].astype(BF16), p["lru_b_x"], p["lru_lambda"])
    n_conv = p["lru_conv_w"].shape[0] - 1
    ob_p, hl_p = rglru(segf, 0, ub_col, 1, seq, jnp.zeros((1, width), F32), jnp.zeros((1, n_conv, width), F32), *lru)
    ob_s, hl_s = rglru(segf, seq, ub_col, bsz, sq, st["h"], st["conv"], *lru)
    o_bb = jnp.concatenate([ob_p, ob_s], axis=0)
    u_b = segf[:, cpad:cpad + width]
    conv_p = u_b[seq - n_conv:seq][None]
    conv_s = u_b[seq:].reshape(bsz, sq, width)[:, sq - n_conv:]

    win = BAND_CHUNKS * CHUNK
    tq_c = win
    assert seq % tq_c == 0
    bias_pp, bias_po = band_bias_tables(p["band_rel_bias"], tq_c, win, tq_c, tq_c)
    kcol, vcol = c_heads, 2 * c_heads
    prev_map = lambda col: (lambda i, h: (jnp.maximum(i - 1, 0), col + h))
    own_map = lambda col: (lambda i, h: (i, col + h))
    oc_p = band_attention(qkvc, 0, 0, qkvc, prev_map(kcol), qkvc, prev_map(vcol), qkvc, own_map(kcol),
                          qkvc, own_map(vcol), bias_pp, bias_po, n_heads=c_heads, n_blocks=seq // tq_c,
                          tq=tq_c, prev=win, own=tq_c, mask_first_prev=True)
    own_s = LANE
    k_new = qkvc[seq:, c_heads * LANE:2 * c_heads * LANE].reshape(bsz, sq, c_heads * LANE)
    v_new = qkvc[seq:, 2 * c_heads * LANE:].reshape(bsz, sq, c_heads * LANE)
    pad_own = lambda a: jnp.pad(a, ((0, 0), (0, own_s - sq), (0, 0))).reshape(bsz * own_s, c_heads * LANE)
    bias_sp, bias_so = band_bias_tables(p["band_rel_bias"], sq, win, own_s, sq)
    cache_k = st["bk"].reshape(bsz * win, c_heads * LANE)
    cache_v = st["bv"].reshape(bsz * win, c_heads * LANE)
    batch_map = lambda i, h: (i, h)
    oc_s = band_attention(qkvc, seq, 0, cache_k, batch_map, cache_v, batch_map, pad_own(k_new), batch_map,
                          pad_own(v_new), batch_map, bias_sp, bias_so, n_heads=c_heads, n_blocks=bsz,
                          tq=sq, prev=win, own=own_s, mask_first_prev=False)
    o_cc = jnp.concatenate([oc_p, oc_s], axis=0)
    keep = min(win, seq)
    k_all = qkvc[:, c_heads * LANE:2 * c_heads * LANE]
    v_all = qkvc[:, 2 * c_heads * LANE:]
    bk_p = k_all[seq - keep:seq].reshape(1, keep, c_heads, LANE)
    bv_p = v_all[seq - keep:seq].reshape(1, keep, c_heads, LANE)
    bk_s = k_new.reshape(bsz, sq, c_heads, LANE)
    bv_s = v_new.reshape(bsz, sq, c_heads, LANE)

    n_grp, n_state = p["s5_a_re"].shape
    abar_re, abar_im, wb, wc = s5_params(p["s5_a_re"], p["s5_a_im"], p["s5_log_dt"], p["s5_b_re"], p["s5_b_im"],
                                         p["s5_c_re"], p["s5_c_im"], LANE)
    pack0 = lambda re, im: jnp.concatenate([re.reshape(-1, 1, n_grp * n_state), im.reshape(-1, 1, n_grp * n_state)],
                                           axis=2)
    zeros0 = jnp.zeros((1, n_grp, n_state), F32)
    z_p, xl_p = s5_scan(segf, 0, ud_col, 1, seq, pack0(zeros0, zeros0), abar_re, abar_im, wb, wc, p["s5_d"])
    z_s, xl_s = s5_scan(segf, seq, ud_col, bsz, sq, pack0(st["s5re"], st["s5im"]), abar_re, abar_im, wb, wc,
                        p["s5_d"])
    z = jnp.concatenate([z_p, z_s], axis=0)
    tm_g = _tile(t_all, 1024)
    tn_g = _tile(width, 1024)
    o_dd = matmul(z.astype(BF16), p["s5_w_glu"].astype(BF16), out_dtype=BF16, tm=tm_g, tn=tn_g, epilogue=_ep_glu,
                  extra=[(z, (tm_g, tn_g), lambda i, j: (i, j)),
                         (p["s5_b_glu"].reshape(1, -1), (1, tn_g), lambda i, j: (0, j))], name="s5_glu")
    unpack = lambda xl, part: xl[:, part * n_grp * n_state:(part + 1) * n_grp * n_state].reshape(-1, n_grp, n_state)

    merged = merge_branches((o_a, o_bb, o_cc, o_dd), p["w_branch"].astype(BF16), gates)
    y = matmul(merged, p["w_out"].astype(BF16), out_dtype=F32, name="w_out")
    x = add_rmsnorm(x, y, g[3], 1.0)
    x = ffn(x, 1, g[4], g[5])

    new_p = (ckv[:seq][None], kr[:seq, :cfg["a_rope"]][None], hl_p, conv_p, bk_p, bv_p, unpack(xl_p, 0),
             unpack(xl_p, 1))
    new_s = (ckv[seq:].reshape(bsz, sq, -1), kr[seq:, :cfg["a_rope"]].reshape(bsz, sq, -1), hl_s, conv_s, bk_s, bv_s,
             unpack(xl_s, 0), unpack(xl_s, 1))
    return x, new_p, new_s


def kernel(x_prompt, x_sample, cache_mla_ckv, cache_mla_krope, state_lru_h, state_lru_conv, cache_band_k, cache_band_v, state_s5_re, state_s5_im, norm_g, ffn_w_up, ffn_w_down, w_in, mla_q_norm, mla_kv_norm, mla_w_uq, mla_w_ukv, lru_conv_w, lru_conv_b, lru_w_a, lru_b_a, lru_w_x, lru_b_x, lru_lambda, band_rel_bias, s5_a_re, s5_a_im, s5_log_dt, s5_b_re, s5_b_im, s5_c_re, s5_c_im, s5_d, s5_w_glu, s5_b_glu, w_branch, w_out):
    depth = norm_g.shape[0]
    n_p, seq, d = x_prompt.shape
    bsz, sq, _ = x_sample.shape
    past = cache_mla_ckv.shape[2]
    a_rope = cache_mla_krope.shape[3]
    mix_w = w_branch.shape[2]
    a_heads = mix_w // LANE
    c_heads = cache_band_k.shape[3]
    assert n_p == 1 and a_rope == 2 * ROPE_HALF and cache_band_k.shape[4] == LANE
    assert mla_w_uq.shape[2] == a_heads * (LANE + a_rope) and mla_w_ukv.shape[2] == a_heads * 2 * LANE
    assert past % CHUNK == 0 and cache_band_k.shape[2] == BAND_CHUNKS * CHUNK and past >= BAND_CHUNKS * CHUNK
    assert sq <= CHUNK and seq % (BAND_CHUNKS * CHUNK) == 0 and sq >= lru_conv_w.shape[1] - 1
    cfg = dict(seq=seq, bsz=bsz, sq=sq, past=past, mix_w=mix_w, a_heads=a_heads, c_heads=c_heads,
               q_lora=mla_q_norm.shape[1], kv_lora=mla_kv_norm.shape[1], a_rope=a_rope)

    pos = np.concatenate([np.arange(seq), np.tile(past + np.arange(sq), bsz)]).astype(np.float32)
    half = a_rope // 2
    inv = ROPE_THETA ** (-jnp.arange(half, dtype=F32) / half)
    ang = jnp.asarray(pos)[:, None] * inv
    zpad = jnp.zeros((pos.shape[0], LANE - a_rope), F32)
    cos = jnp.concatenate([jnp.cos(ang), jnp.cos(ang), zpad], axis=1)
    sin = jnp.concatenate([-jnp.sin(ang), jnp.sin(ang), zpad], axis=1)

    x = jnp.concatenate([x_prompt.reshape(seq, d), x_sample.reshape(bsz * sq, d)], axis=0)
    new_p, new_s = [], []
    for l in range(depth):
        st = dict(ckv=cache_mla_ckv[l], kr=cache_mla_krope[l], h=state_lru_h[l], conv=state_lru_conv[l],
                  bk=cache_band_k[l], bv=cache_band_v[l], s5re=state_s5_re[l], s5im=state_s5_im[l])
        p = dict(norm_g=norm_g[l], ffn_w_up=ffn_w_up[l], ffn_w_down=ffn_w_down[l], w_in=w_in[l],
                 mla_q_norm=mla_q_norm[l], mla_kv_norm=mla_kv_norm[l], mla_w_uq=mla_w_uq[l], mla_w_ukv=mla_w_ukv[l],
                 lru_conv_w=lru_conv_w[l], lru_conv_b=lru_conv_b[l], lru_w_a=lru_w_a[l], lru_b_a=lru_b_a[l],
                 lru_w_x=lru_w_x[l], lru_b_x=lru_b_x[l], lru_lambda=lru_lambda[l], band_rel_bias=band_rel_bias[l],
                 s5_a_re=s5_a_re[l], s5_a_im=s5_a_im[l], s5_log_dt=s5_log_dt[l], s5_b_re=s5_b_re[l],
                 s5_b_im=s5_b_im[l], s5_c_re=s5_c_re[l], s5_c_im=s5_c_im[l], s5_d=s5_d[l], s5_w_glu=s5_w_glu[l],
                 s5_b_glu=s5_b_glu[l], w_branch=w_branch[l], w_out=w_out[l])
        x, st_p, st_s = _layer(x, cfg, cos, sin, st, p)
        new_p.append(st_p)
        new_s.append(st_s)

    outs = [x[:seq].reshape(1, seq, d), x[seq:].reshape(bsz, sq, d)]
    for i in range(8):
        outs.append(jnp.stack([s[i] for s in new_p], axis=0))
        outs.append(jnp.stack([s[i] for s in new_s], axis=0))
    return tuple(outs)
```

```python
import functools
import math

import jax
import jax.numpy as jnp
import numpy as np
from jax import lax
from jax.experimental import pallas as pl
from jax.experimental.pallas import tpu as pltpu

F32 = jnp.float32
BF16 = jnp.bfloat16

CHUNK = 64
BAND_CHUNKS = 8
EPS = 1e-6
ROPE_THETA = 10000.0
LRU_C = 8.0
ROPE_HALF = 32
MASK_VALUE = -1e30
LANE = 128
SUBLANE = 8
VMEM_LIMIT = 56 * 1024 * 1024


def _cparams(*sem):
    return pltpu.CompilerParams(dimension_semantics=sem, vmem_limit_bytes=VMEM_LIMIT)


def _round_up(x, m):
    return (x + m - 1) // m * m


def _tile(dim, pref):
    if dim <= pref:
        return dim
    t = pref
    while dim % t:
        t //= 2
    assert t >= SUBLANE, (dim, pref)
    return t


def _rmsnorm_kernel(x_ref, g_ref, o_ref):
    x = x_ref[...]
    ms = jnp.mean(x * x, axis=-1, keepdims=True)
    o_ref[...] = (x * lax.rsqrt(ms + EPS) * g_ref[...]).astype(o_ref.dtype)


def rmsnorm(x, g, out_dtype=BF16):
    t, d = x.shape
    tm = _tile(t, 256)
    return pl.pallas_call(
        _rmsnorm_kernel,
        grid=(t // tm,),
        in_specs=[pl.BlockSpec((tm, d), lambda i: (i, 0)), pl.BlockSpec((1, d), lambda i: (0, 0))],
        out_specs=pl.BlockSpec((tm, d), lambda i: (i, 0)),
        out_shape=jax.ShapeDtypeStruct((t, d), out_dtype),
        compiler_params=_cparams("parallel"),
        name="rmsnorm",
    )(x, g.reshape(1, d))


def _add_rmsnorm_kernel(x_ref, y_ref, g_ref, *rest, scale, with_next):
    y = y_ref[...]
    ms = jnp.mean(y * y, axis=-1, keepdims=True)
    x = x_ref[...] + scale * (y * lax.rsqrt(ms + EPS) * g_ref[...])
    if not with_next:
        rest[0][...] = x
        return
    gn_ref, o_ref, h_ref = rest
    o_ref[...] = x
    ms = jnp.mean(x * x, axis=-1, keepdims=True)
    h_ref[...] = (x * lax.rsqrt(ms + EPS) * gn_ref[...]).astype(h_ref.dtype)


def add_rmsnorm(x, y, g, scale, g_next=None):
    t, d = x.shape
    tm = _tile(t, 256)
    row = pl.BlockSpec((tm, d), lambda i: (i, 0))
    vec = pl.BlockSpec((1, d), lambda i: (0, 0))
    with_next = g_next is not None
    return pl.pallas_call(
        functools.partial(_add_rmsnorm_kernel, scale=scale, with_next=with_next),
        grid=(t // tm,),
        in_specs=[row, row, vec] + ([vec] if with_next else []),
        out_specs=[row, row] if with_next else row,
        out_shape=([jax.ShapeDtypeStruct((t, d), F32), jax.ShapeDtypeStruct((t, d), BF16)] if with_next
                   else jax.ShapeDtypeStruct((t, d), F32)),
        compiler_params=_cparams("parallel"),
        name="add_rmsnorm",
    )(x, y, g.reshape(1, d), *([g_next.reshape(1, d)] if with_next else []))


def _k_tile(k):
    if k <= 4096:
        return k
    for parts in range(2, k // LANE + 1):
        if k % parts == 0 and (k // parts) % LANE == 0 and k // parts <= 4096:
            return k // parts
    raise ValueError(k)


def _mm_kernel(*refs, nk, n_extra, epilogue):
    a_ref, b_ref = refs[0], refs[1]
    extra = refs[2:2 + n_extra]
    o_ref = refs[2 + n_extra]
    if nk == 1:
        acc = jnp.dot(a_ref[...], b_ref[...], preferred_element_type=F32)
        o_ref[...] = epilogue(acc, *extra).astype(o_ref.dtype)
        return
    acc_ref = refs[3 + n_extra]
    k = pl.program_id(2)

    @pl.when(k == 0)
    def _():
        acc_ref[...] = jnp.zeros_like(acc_ref)

    acc_ref[...] += jnp.dot(a_ref[...], b_ref[...], preferred_element_type=F32)

    @pl.when(k == nk - 1)
    def _():
        o_ref[...] = epilogue(acc_ref[...], *extra).astype(o_ref.dtype)


def _ep_none(acc):
    return acc


def _ep_sigmoid(acc):
    return jax.nn.sigmoid(acc)


def _ep_glu(acc, z_ref, b_ref):
    z = z_ref[...]
    return z * jax.nn.sigmoid(acc + b_ref[...])


def _ep_pack_kv(acc, kr_ref):
    kr = kr_ref[...].astype(F32)
    ones = jnp.ones_like(kr)
    outs = []
    for h in range(acc.shape[1] // (2 * LANE)):
        outs += [acc[:, h * 2 * LANE:h * 2 * LANE + LANE], kr, acc[:, h * 2 * LANE + LANE:(h + 1) * 2 * LANE], ones]
    return jnp.concatenate(outs, axis=1)


def _ep_rope_q(acc, cos_ref, sin_ref, *, scale, half):
    cos, sin = cos_ref[...], sin_ref[...]
    lane = lax.broadcasted_iota(jnp.int32, cos.shape, 1)
    outs = []
    for h in range(acc.shape[1] // (2 * LANE)):
        nope = acc[:, h * 2 * LANE:h * 2 * LANE + LANE]
        r = acc[:, h * 2 * LANE + LANE:(h + 1) * 2 * LANE]
        swapped = jnp.where(lane < half, pltpu.roll(r, LANE - half, axis=1), pltpu.roll(r, half, axis=1))
        outs += [nope * scale, (r * cos + swapped * sin) * scale]
    return jnp.concatenate(outs, axis=1)


def matmul(a, b, *, out_dtype, m=None, tm=1024, tn=1024, epilogue=_ep_none, extra=(), widen=1, name="matmul"):
    m = a.shape[0] if m is None else m
    k, n = b.shape
    assert a.shape[1] == k
    tm, tn, tk = _tile(m, tm), _tile(n, tn), _k_tile(k)
    nk = k // tk
    if nk == 1:
        grid = (m // tm, n // tn)
        wrap = lambda f: f
        a_spec = pl.BlockSpec((tm, k), lambda i, j: (i, 0))
        b_spec = pl.BlockSpec((k, tn), lambda i, j: (0, j))
        o_spec = pl.BlockSpec((tm, tn * widen), lambda i, j: (i, j))
        scratch = []
        sem = ("parallel", "parallel")
    else:
        grid = (m // tm, n // tn, nk)
        wrap = lambda f: (lambda i, j, kk: f(i, j))
        a_spec = pl.BlockSpec((tm, tk), lambda i, j, kk: (i, kk))
        b_spec = pl.BlockSpec((tk, tn), lambda i, j, kk: (kk, j))
        o_spec = pl.BlockSpec((tm, tn * widen), lambda i, j, kk: (i, j))
        scratch = [pltpu.VMEM((tm, tn), F32)]
        sem = ("parallel", "parallel", "arbitrary")
    extra_specs = [pl.BlockSpec(bs, wrap(im)) for _, bs, im in extra]
    return pl.pallas_call(
        functools.partial(_mm_kernel, nk=nk, n_extra=len(extra), epilogue=epilogue),
        grid=grid,
        in_specs=[a_spec, b_spec] + extra_specs,
        out_specs=o_spec,
        out_shape=jax.ShapeDtypeStruct((m, n * widen), out_dtype),
        scratch_shapes=scratch,
        compiler_params=_cparams(*sem),
        name=name,
    )(a, b, *[e[0] for e in extra])


def _swiglu_kernel(a_ref, wg_ref, wu_ref, o_ref):
    a = a_ref[...]
    gate = jnp.dot(a, wg_ref[...], preferred_element_type=F32)
    up = jnp.dot(a, wu_ref[...], preferred_element_type=F32)
    o_ref[...] = (jax.nn.silu(gate) * up).astype(o_ref.dtype)


def swiglu_up(a, w_up, ffp):
    t, d = a.shape
    tm, tn = _tile(t, 1024), _tile(ffp, 512)
    nj = ffp // tn
    return pl.pallas_call(
        _swiglu_kernel,
        grid=(t // tm, nj),
        in_specs=[pl.BlockSpec((tm, d), lambda i, j: (i, 0)),
                  pl.BlockSpec((d, tn), lambda i, j: (0, j)),
                  pl.BlockSpec((d, tn), lambda i, j: (0, j + nj))],
        out_specs=pl.BlockSpec((tm, tn), lambda i, j: (i, j)),
        out_shape=jax.ShapeDtypeStruct((t, ffp), BF16),
        compiler_params=_cparams("parallel", "parallel"),
        name="swiglu_up",
    )(a, w_up, w_up)


def _merge_kernel(oa_ref, ob_ref, oc_ref, od_ref, wb_ref, ga_ref, gb_ref, gc_ref, gd_ref, o_ref):
    total = None
    for i, (o_r, g_r) in enumerate(((oa_ref, ga_ref), (ob_ref, gb_ref), (oc_ref, gc_ref), (od_ref, gd_ref))):
        part = g_r[...].astype(F32) * jnp.dot(o_r[...], wb_ref[i], preferred_element_type=F32)
        total = part if total is None else total + part
    o_ref[...] = total.astype(o_ref.dtype)


def merge_branches(outs, w_branch, gates):
    t, w = outs[0].shape
    d = w_branch.shape[2]
    tm, tn = _tile(t, 512), _tile(d, 512)
    nj = d // tn
    o_spec = pl.BlockSpec((tm, w), lambda i, j: (i, 0))
    g_specs = [pl.BlockSpec((tm, tn), functools.partial(lambda i, j, b: (i, j + b * nj), b=b)) for b in range(4)]
    return pl.pallas_call(
        _merge_kernel,
        grid=(t // tm, nj),
        in_specs=[o_spec] * 4 + [pl.BlockSpec((4, w, tn), lambda i, j: (0, 0, j))] + g_specs,
        out_specs=pl.BlockSpec((tm, tn), lambda i, j: (i, j)),
        out_shape=jax.ShapeDtypeStruct((t, d), BF16),
        compiler_params=_cparams("parallel", "parallel"),
        name="merge_branches",
    )(*outs, w_branch, gates, gates, gates, gates)


def _mla_prep_kernel(c_ref, qn_ref, kvn_ref, cos_ref, sin_ref, cq_ref, ckv_ref, ckvb_ref, kr_ref, krb_ref,
                     *, q_lora, kv_lora, half):
    c_q = c_ref[:, :q_lora]
    ms = jnp.mean(c_q * c_q, axis=-1, keepdims=True)
    cq_ref[...] = (c_q * lax.rsqrt(ms + EPS) * qn_ref[...]).astype(cq_ref.dtype)
    c_kv = c_ref[:, q_lora:q_lora + kv_lora]
    ms = jnp.mean(c_kv * c_kv, axis=-1, keepdims=True)
    ckv = c_kv * lax.rsqrt(ms + EPS) * kvn_ref[...]
    ckv_ref[...] = ckv
    ckvb_ref[...] = ckv.astype(ckvb_ref.dtype)
    r = c_ref[:, q_lora + kv_lora:q_lora + kv_lora + LANE]
    lane = lax.broadcasted_iota(jnp.int32, r.shape, 1)
    swapped = jnp.where(lane < half, pltpu.roll(r, LANE - half, axis=1), pltpu.roll(r, half, axis=1))
    kr = r * cos_ref[...] + swapped * sin_ref[...]
    kr_ref[...] = kr
    krb_ref[...] = kr.astype(krb_ref.dtype)


def mla_prep(segf, q_norm, kv_norm, cos, sin, cpad):
    t = segf.shape[0]
    q_lora, kv_lora = q_norm.shape[0], kv_norm.shape[0]
    tm = _tile(t, 256)
    row = lambda w: pl.BlockSpec((tm, w), lambda i: (i, 0))
    const = lambda w: pl.BlockSpec((1, w), lambda i: (0, 0))
    return pl.pallas_call(
        functools.partial(_mla_prep_kernel, q_lora=q_lora, kv_lora=kv_lora, half=ROPE_HALF),
        grid=(t // tm,),
        in_specs=[row(cpad), const(q_lora), const(kv_lora), row(LANE), row(LANE)],
        out_specs=[row(q_lora), row(kv_lora), row(kv_lora), row(LANE), row(LANE)],
        out_shape=[jax.ShapeDtypeStruct((t, q_lora), BF16), jax.ShapeDtypeStruct((t, kv_lora), F32),
                   jax.ShapeDtypeStruct((t, kv_lora), BF16), jax.ShapeDtypeStruct((t, LANE), F32),
                   jax.ShapeDtypeStruct((t, LANE), BF16)],
        compiler_params=_cparams("parallel"),
        name="mla_prep",
    )(segf, q_norm.reshape(1, -1), kv_norm.reshape(1, -1), cos, sin)


def _mla_flash_kernel(qi_ref, ki_ref, q_ref, kv_ref, o_ref, m_ref, acc_ref, *, heads, tq, tk):
    s_id = pl.program_id(1)
    qi, ki = qi_ref[s_id], ki_ref[s_id]
    hw = 2 * LANE

    @pl.when(ki == 0)
    def _():
        m_ref[...] = jnp.full_like(m_ref, -jnp.inf)
        acc_ref[...] = jnp.zeros_like(acc_ref)

    def step(masked):
        if masked:
            rc = lax.broadcasted_iota(jnp.int32, (tq, tk), 0) // CHUNK
            cc = lax.broadcasted_iota(jnp.int32, (tq, tk), 1) // CHUNK
            visible = cc <= rc
        for h in range(heads):
            q = q_ref[:, h * hw:(h + 1) * hw]
            k = kv_ref[:, 2 * h * hw:(2 * h + 1) * hw]
            v1 = kv_ref[:, (2 * h + 1) * hw:(2 * h + 2) * hw]
            s = lax.dot_general(q, k, (((1,), (1,)), ((), ())), preferred_element_type=F32)
            if masked:
                s = jnp.where(visible, s, MASK_VALUE)
            m_prev = m_ref[h]
            m_next = jnp.maximum(m_prev, jnp.max(s, axis=1)[:, None])
            p = jnp.exp2(s - jnp.tile(m_next, (1, tk // LANE)))
            alpha = jnp.exp2(m_prev - m_next)
            m_ref[h] = m_next
            pv = jnp.dot(p.astype(BF16), v1, preferred_element_type=F32)
            acc_ref[:, h * hw:(h + 1) * hw] = jnp.tile(alpha, (1, 2)) * acc_ref[:, h * hw:(h + 1) * hw] + pv

    @pl.when(ki < qi)
    def _():
        step(False)

    @pl.when(ki == qi)
    def _():
        step(True)
        for h in range(heads):
            o_ref[:, h * LANE:(h + 1) * LANE] = (
                acc_ref[:, h * hw:h * hw + LANE] / acc_ref[:, h * hw + LANE:(h + 1) * hw]).astype(o_ref.dtype)


def mla_flash(q, kv, seq, n_heads, *, tile=1024, heads_per_step=4):
    tq = tk = _tile(seq, tile)
    hg = min(heads_per_step, n_heads)
    assert n_heads % hg == 0 and tq % CHUNK == 0
    nq = seq // tq
    pairs = [(i, j) for i in range(nq) for j in range(i + 1)]
    qi_arr = jnp.asarray(np.array([p[0] for p in pairs], np.int32))
    ki_arr = jnp.asarray(np.array([p[1] for p in pairs], np.int32))
    grid_spec = pltpu.PrefetchScalarGridSpec(
        num_scalar_prefetch=2,
        grid=(n_heads // hg, len(pairs)),
        in_specs=[pl.BlockSpec((tq, hg * 2 * LANE), lambda g, s, qi, ki: (qi[s], g)),
                  pl.BlockSpec((tk, hg * 4 * LANE), lambda g, s, qi, ki: (ki[s], g))],
        out_specs=pl.BlockSpec((tq, hg * LANE), lambda g, s, qi, ki: (qi[s], g)),
        scratch_shapes=[pltpu.VMEM((hg, tq, LANE), F32), pltpu.VMEM((tq, hg * 2 * LANE), F32)],
    )
    return pl.pallas_call(
        functools.partial(_mla_flash_kernel, heads=hg, tq=tq, tk=tk),
        grid_spec=grid_spec,
        out_shape=jax.ShapeDtypeStruct((seq, n_heads * LANE), BF16),
        compiler_params=_cparams("parallel", "arbitrary"),
        name="mla_flash",
    )(qi_arr, ki_arr, q, kv)


def _mla_sample_kernel(q_ref, kv_ref, kr_ref, o_ref, *, heads, n_keys, past):
    sq, kp = q_ref.shape[0], kv_ref.shape[0]
    kr = kr_ref[...]
    qpos = past + lax.broadcasted_iota(jnp.int32, (sq, kp), 0)
    kpos = lax.broadcasted_iota(jnp.int32, (sq, kp), 1)
    visible = (kpos < n_keys) & (kpos // CHUNK <= qpos // CHUNK)
    for h in range(heads):
        q = q_ref[:, h * 2 * LANE:(h + 1) * 2 * LANE]
        k = jnp.concatenate([kv_ref[:, h * 2 * LANE:h * 2 * LANE + LANE], kr], axis=1)
        v = kv_ref[:, h * 2 * LANE + LANE:(h + 1) * 2 * LANE]
        s = lax.dot_general(q, k, (((1,), (1,)), ((), ())), preferred_element_type=F32)
        s = jnp.where(visible, s, MASK_VALUE)
        p = jnp.exp2(s - jnp.max(s, axis=-1, keepdims=True))
        denom = jnp.sum(p, axis=-1, keepdims=True)
        pv = jnp.dot(p.astype(v.dtype), v, preferred_element_type=F32)
        o_ref[:, h * LANE:(h + 1) * LANE] = (pv / denom).astype(o_ref.dtype)


def mla_sample(q, kv, kr, row0, bsz, sq, n_heads, n_keys, past):
    kp = kv.shape[0] // bsz
    assert row0 % sq == 0
    return pl.pallas_call(
        functools.partial(_mla_sample_kernel, heads=n_heads, n_keys=n_keys, past=past),
        grid=(bsz,),
        in_specs=[pl.BlockSpec((sq, n_heads * 2 * LANE), lambda b: (row0 // sq + b, 0)),
                  pl.BlockSpec((kp, n_heads * 2 * LANE), lambda b: (b, 0)),
                  pl.BlockSpec((kp, LANE), lambda b: (b, 0))],
        out_specs=pl.BlockSpec((sq, n_heads * LANE), lambda b: (b, 0)),
        out_shape=jax.ShapeDtypeStruct((bsz * sq, n_heads * LANE), BF16),
        compiler_params=_cparams("parallel"),
        name="mla_sample",
    )(q, kv, kr)


def _band_kernel(q_ref, kp_ref, ko_ref, vp_ref, vo_ref, bp_ref, bo_ref, o_ref, *, heads, scale, mask_first_prev):
    dn = (((1,), (1,)), ((), ()))
    ones_p = jnp.ones((kp_ref.shape[0], LANE), BF16)
    ones_o = jnp.ones((ko_ref.shape[0], LANE), BF16)
    for h in range(heads):
        cs = slice(h * LANE, (h + 1) * LANE)
        q = (q_ref[:, cs] * scale).astype(BF16)
        sp = lax.dot_general(q, kp_ref[:, cs].astype(BF16), dn, preferred_element_type=F32) + bp_ref[h]
        so = lax.dot_general(q, ko_ref[:, cs].astype(BF16), dn, preferred_element_type=F32) + bo_ref[h]
        if mask_first_prev:
            sp = jnp.where(pl.program_id(1) > 0, sp, MASK_VALUE)
        m = jnp.maximum(jnp.max(sp, axis=-1, keepdims=True), jnp.max(so, axis=-1, keepdims=True))
        pp, po = jnp.exp2(sp - m), jnp.exp2(so - m)
        vp1 = jnp.concatenate([vp_ref[:, cs].astype(BF16), ones_p], axis=1)
        vo1 = jnp.concatenate([vo_ref[:, cs].astype(BF16), ones_o], axis=1)
        pv = (jnp.dot(pp.astype(BF16), vp1, preferred_element_type=F32)
              + jnp.dot(po.astype(BF16), vo1, preferred_element_type=F32))
        o_ref[:, cs] = (pv[:, :LANE] / pv[:, LANE:]).astype(o_ref.dtype)


def band_attention(q_arr, q_row0, q_col0, kp_arr, kp_map, vp_arr, vp_map, ko_arr, ko_map, vo_arr, vo_map,
                   bias_p, bias_o, *, n_heads, hps, n_blocks, tq, prev, own, mask_first_prev):
    hd = LANE
    w = hps * hd
    assert q_row0 % tq == 0 and n_heads % hps == 0
    return pl.pallas_call(
        functools.partial(_band_kernel, heads=hps, scale=hd ** -0.5 * math.log2(math.e),
                          mask_first_prev=mask_first_prev),
        grid=(n_heads // hps, n_blocks),
        in_specs=[pl.BlockSpec((tq, w), lambda g, i: (q_row0 // tq + i, q_col0 + g)),
                  pl.BlockSpec((prev, w), lambda g, i: kp_map(i, g)),
                  pl.BlockSpec((own, w), lambda g, i: ko_map(i, g)),
                  pl.BlockSpec((prev, w), lambda g, i: vp_map(i, g)),
                  pl.BlockSpec((own, w), lambda g, i: vo_map(i, g)),
                  pl.BlockSpec((hps, tq, prev), lambda g, i: (g, 0, 0)),
                  pl.BlockSpec((hps, tq, own), lambda g, i: (g, 0, 0))],
        out_specs=pl.BlockSpec((tq, w), lambda g, i: (i, g)),
        out_shape=jax.ShapeDtypeStruct((n_blocks * tq, n_heads * hd), BF16),
        compiler_params=_cparams("parallel", "arbitrary"),
        name="band_attention",
    )(q_arr, kp_arr, ko_arr, vp_arr, vo_arr, bias_p, bias_o)


def band_bias_tables(rel_bias, tq, prev, own, own_valid):
    n_heads = rel_bias.shape[0]
    clip = (rel_bias.shape[1] - 1) // 2
    qr = np.arange(tq)[:, None]

    def table(krel, valid):
        nk, k0 = krel.shape[1], int(krel[0, 0])
        period = nk + tq
        e = np.zeros(period, np.int64)
        e[:nk] = -np.arange(nk)
        e[nk + 1:] = np.arange(tq - 1, 0, -1)
        per_offset = jnp.take(rel_bias, jnp.asarray(np.clip(e - k0, -clip, clip) + clip), axis=1)
        toeplitz = jnp.tile(per_offset, (1, tq))[:, :tq * (period - 1)].reshape(n_heads, tq, period - 1)[:, :, :nk]
        dc = qr // CHUNK - np.floor_divide(krel, CHUNK)
        vis = (dc >= 0) & (dc <= BAND_CHUNKS) & valid
        return jnp.where(jnp.asarray(vis)[None], toeplitz * math.log2(math.e), MASK_VALUE)

    kp = np.arange(-prev, 0)[None, :]
    ko = np.arange(own)[None, :]
    return table(kp, np.ones_like(kp, bool)), table(ko, ko < own_valid)


def _rglru_kernel(u_ref, h0_ref, c0_ref, cw_ref, cb_ref, wa_ref, ba_ref, wx_ref, bx_ref, lam_ref,
                  o_ref, hl_ref, prev_ref, h_ref, a_ref, b_ref, *, n_blocks, bw):
    t = pl.program_id(1)
    tb, width = u_ref.shape

    @pl.when(t == 0)
    def _():
        prev_ref[...] = c0_ref[0]
        h_ref[...] = h0_ref[0]

    u = u_ref[...]
    cw = cw_ref[...]
    nw = 4
    xc = cb_ref[...] + cw[nw - 1:nw, :] * u
    for s in range(1, nw):
        xc = xc + cw[nw - 1 - s:nw - s, :] * pltpu.roll(u, s, axis=0)
    head = jnp.concatenate([prev_ref[...], u[:SUBLANE, :]], axis=0)
    xh = cb_ref[...] + cw[nw - 1:nw, :] * u[:SUBLANE, :]
    for s in range(1, nw):
        xh = xh + cw[nw - 1 - s:nw - s, :] * pltpu.roll(head, s, axis=0)[SUBLANE:, :]
    a_ref[:SUBLANE, :] = xh
    a_ref[SUBLANE:, :] = xc[SUBLANE:, :]
    xc = a_ref[...]
    prev_ref[...] = u[tb - SUBLANE:, :]

    neg_c_sp = -LRU_C * jax.nn.softplus(-lam_ref[...])
    for n in range(n_blocks):
        sl = slice(n * bw, (n + 1) * bw)
        xb = xc[:, sl].astype(BF16)
        r = jax.nn.sigmoid(jnp.dot(xb, wa_ref[n], preferred_element_type=F32) + ba_ref[:, sl])
        i = jax.nn.sigmoid(jnp.dot(xb, wx_ref[n], preferred_element_type=F32) + bx_ref[:, sl])
        log_a = neg_c_sp[:, sl] * r
        a = jnp.exp(log_a)
        a_ref[:, sl] = a
        b_ref[:, sl] = jnp.sqrt(-jnp.tanh(log_a) * (a * a + 1.0)) * (i * xc[:, sl])

    def row(j, h):
        h = a_ref[pl.ds(j, 1), :] * h + b_ref[pl.ds(j, 1), :]
        b_ref[pl.ds(j, 1), :] = h
        return h

    h = lax.fori_loop(0, tb, row, h_ref[...])
    h_ref[...] = h
    o_ref[...] = b_ref[...].astype(o_ref.dtype)
    hl_ref[0] = h


def rglru(u_arr, row0, col0, n_seq, seq, h0, conv0, conv_w, conv_b, w_a, b_a, w_x, b_x, lam):
    width = h0.shape[-1]
    n_blocks, bw = w_a.shape[0], w_a.shape[1]
    tb = _tile(seq, 256)
    nt = seq // tb
    assert row0 % tb == 0 and tb >= 2 * SUBLANE and conv_w.shape[0] == 4
    conv0p = jnp.concatenate([jnp.zeros((n_seq, SUBLANE - conv0.shape[1], width), F32), conv0], axis=1)
    cwp = jnp.concatenate([conv_w, jnp.zeros((SUBLANE - conv_w.shape[0], width), F32)], axis=0)
    vec = lambda: pl.BlockSpec((1, width), lambda s, t: (0, 0))
    wspec = lambda: pl.BlockSpec((n_blocks, bw, bw), lambda s, t: (0, 0, 0))
    out, hl = pl.pallas_call(
        functools.partial(_rglru_kernel, n_blocks=n_blocks, bw=bw),
        grid=(n_seq, nt),
        in_specs=[pl.BlockSpec((tb, width), lambda s, t: (row0 // tb + s * nt + t, col0)),
                  pl.BlockSpec((1, 1, width), lambda s, t: (s, 0, 0)),
                  pl.BlockSpec((1, SUBLANE, width), lambda s, t: (s, 0, 0)),
                  pl.BlockSpec((SUBLANE, width), lambda s, t: (0, 0)),
                  vec(), wspec(), vec(), wspec(), vec(), vec()],
        out_specs=[pl.BlockSpec((tb, width), lambda s, t: (s * nt + t, 0)),
                   pl.BlockSpec((1, 1, width), lambda s, t: (s, 0, 0))],
        out_shape=[jax.ShapeDtypeStruct((n_seq * seq, width), BF16),
                   jax.ShapeDtypeStruct((n_seq, 1, width), F32)],
        scratch_shapes=[pltpu.VMEM((SUBLANE, width), F32), pltpu.VMEM((1, width), F32),
                        pltpu.VMEM((tb, width), F32), pltpu.VMEM((tb, width), F32)],
        compiler_params=_cparams("parallel", "arbitrary"),
        name="rglru",
    )(u_arr, h0.reshape(n_seq, 1, width), conv0p, cwp, conv_b.reshape(1, -1), w_a, b_a.reshape(1, -1),
      w_x, b_x.reshape(1, -1), lam.reshape(1, -1))
    return out, hl.reshape(n_seq, width)


def _permute_rows(perm, x):
    hi = x.astype(BF16)
    rest = x - hi.astype(F32)
    mid = rest.astype(BF16)
    lo = (rest - mid.astype(F32)).astype(BF16)
    dot = lambda piece: jnp.dot(perm, piece, preferred_element_type=F32)
    return (dot(hi) + dot(mid)) + dot(lo)


def _s5_kernel(u_ref, x0_ref, pre_ref, pim_ref, wb_ref, wc_ref, d_ref, z_ref, xl_ref, up_ref, xs_ref, xc_ref,
               *, n_kb, strip):
    t = pl.program_id(1)
    tb, width = u_ref.shape
    seg = tb // SUBLANE
    half = xs_ref.shape[1] // 2
    kw = width // n_kb
    sw = half // n_kb

    @pl.when(t == 0)
    def _():
        xc_ref[...] = x0_ref[0]

    idx0 = lax.broadcasted_iota(jnp.int32, (tb, tb), 0)
    idx1 = lax.broadcasted_iota(jnp.int32, (tb, tb), 1)
    to_segments = jnp.where(idx1 == (idx0 % SUBLANE) * seg + idx0 // SUBLANE, 1.0, 0.0).astype(BF16)
    to_time = jnp.where(idx0 == (idx1 % SUBLANE) * seg + idx1 // SUBLANE, 1.0, 0.0).astype(BF16)
    ub = jnp.dot(to_segments, u_ref[...].astype(BF16), preferred_element_type=F32).astype(BF16)
    for kb in range(n_kb):
        bu = jnp.dot(ub[:, kb * kw:(kb + 1) * kw], wb_ref[kb], preferred_element_type=F32)
        xs_ref[:, kb * sw:(kb + 1) * sw] = bu[:, :sw]
        xs_ref[:, half + kb * sw:half + (kb + 1) * sw] = bu[:, sw:]

    for c in range(half // strip):
        re = slice(c * strip, (c + 1) * strip)
        im = slice(half + c * strip, half + (c + 1) * strip)
        a_re = jnp.broadcast_to(pre_ref[0:1, re], (SUBLANE, strip))
        a_im = jnp.broadcast_to(pim_ref[0:1, re], (SUBLANE, strip))

        def local(j, carry):
            x_re, x_im = carry
            rows = pl.ds(pl.multiple_of(j * SUBLANE, SUBLANE), SUBLANE)
            n_re = a_re * x_re - a_im * x_im + xs_ref[rows, re]
            n_im = a_re * x_im + a_im * x_re + xs_ref[rows, im]
            xs_ref[rows, re] = n_re
            xs_ref[rows, im] = n_im
            return n_re, n_im

        zero = jnp.zeros((SUBLANE, strip), F32)
        e_re, e_im = lax.fori_loop(0, seg, local, (zero, zero), unroll=True)

        al_re, al_im = pre_ref[seg - 1:seg, re], pim_ref[seg - 1:seg, re]
        c_re, c_im = xc_ref[:, re], xc_ref[:, im]
        ins_re, ins_im = [], []
        for s in range(SUBLANE):
            ins_re.append(c_re)
            ins_im.append(c_im)
            c_re, c_im = (al_re * c_re - al_im * c_im + e_re[s:s + 1, :],
                          al_re * c_im + al_im * c_re + e_im[s:s + 1, :])
        xc_ref[:, re] = c_re
        xc_ref[:, im] = c_im
        in_re = jnp.concatenate(ins_re, axis=0)
        in_im = jnp.concatenate(ins_im, axis=0)

        def fix(j, carry):
            rows = pl.ds(pl.multiple_of(j * SUBLANE, SUBLANE), SUBLANE)
            w_re, w_im = pre_ref[pl.ds(j, 1), re], pim_ref[pl.ds(j, 1), re]
            xs_ref[rows, re] = xs_ref[rows, re] + (w_re * in_re - w_im * in_im)
            xs_ref[rows, im] = xs_ref[rows, im] + (w_re * in_im + w_im * in_re)
            return carry

        lax.fori_loop(0, seg, fix, 0, unroll=True)

    xl_ref[0] = xc_ref[...]
    for kb in range(n_kb):
        xb = jnp.concatenate([xs_ref[:, kb * sw:(kb + 1) * sw],
                              xs_ref[:, half + kb * sw:half + (kb + 1) * sw]], axis=1).astype(BF16)
        up_ref[:, kb * kw:(kb + 1) * kw] = jnp.dot(xb, wc_ref[kb], preferred_element_type=F32)
    y = _permute_rows(to_time, up_ref[...])
    z_ref[...] = jax.nn.gelu(y + d_ref[...] * u_ref[...])


def s5_scan(u_arr, row0, col0, n_seq, seq, x0, ld_re, ld_im, wb, wc, d_skip):
    n_kb, kw, two_sw = wb.shape
    width = n_kb * kw
    half = n_kb * two_sw // 2
    tb = _tile(seq, 128)
    nt = seq // tb
    seg = tb // SUBLANE
    assert row0 % tb == 0 and tb % SUBLANE == 0
    steps = jnp.arange(1, seg + 1, dtype=F32)[:, None]
    mag = jnp.exp(steps * ld_re)
    pow_re, pow_im = mag * jnp.cos(steps * ld_im), mag * jnp.sin(steps * ld_im)
    z, xl = pl.pallas_call(
        functools.partial(_s5_kernel, n_kb=n_kb, strip=min(1024, half)),
        grid=(n_seq, nt),
        in_specs=[pl.BlockSpec((tb, width), lambda s, t: (row0 // tb + s * nt + t, col0)),
                  pl.BlockSpec((1, 1, 2 * half), lambda s, t: (s, 0, 0)),
                  pl.BlockSpec((seg, half), lambda s, t: (0, 0)),
                  pl.BlockSpec((seg, half), lambda s, t: (0, 0)),
                  pl.BlockSpec(wb.shape, lambda s, t: (0, 0, 0)),
                  pl.BlockSpec(wc.shape, lambda s, t: (0, 0, 0)),
                  pl.BlockSpec((1, width), lambda s, t: (0, 0))],
        out_specs=[pl.BlockSpec((tb, width), lambda s, t: (s * nt + t, 0)),
                   pl.BlockSpec((1, 1, 2 * half), lambda s, t: (s, 0, 0))],
        out_shape=[jax.ShapeDtypeStruct((n_seq * seq, width), F32),
                   jax.ShapeDtypeStruct((n_seq, 1, 2 * half), F32)],
        scratch_shapes=[pltpu.VMEM((tb, width), F32), pltpu.VMEM((tb, 2 * half), F32),
                        pltpu.VMEM((1, 2 * half), F32)],
        compiler_params=_cparams("parallel", "arbitrary"),
        name="s5_scan",
    )(u_arr, x0, pow_re, pow_im, wb, wc, d_skip.reshape(1, -1))
    return z, xl.reshape(n_seq, 2 * half)


def s5_params(a_re, a_im, log_dt, b_re, b_im, c_re, c_im, kw):
    g, n, gc = b_re.shape
    gpb = kw // gc
    n_kb = g // gpb
    dt = jnp.exp(log_dt)[:, None]
    ld_re, ld_im = a_re * dt, a_im * dt
    e = jnp.exp(ld_re)
    abar_re, abar_im = e * jnp.cos(ld_im), e * jnp.sin(ld_im)
    den = a_re * a_re + a_im * a_im
    q_re = ((abar_re - 1.0) * a_re + abar_im * a_im) / den
    q_im = (abar_im * a_re - (abar_re - 1.0) * a_im) / den
    bb_re = q_re[..., None] * b_re - q_im[..., None] * b_im
    bb_im = q_re[..., None] * b_im + q_im[..., None] * b_re
    eye = jnp.eye(gpb, dtype=F32)

    def pack_b(bb):
        bb = bb.reshape(n_kb, gpb, n, gc)
        return jnp.einsum('kgnc,gh->kgchn', bb, eye).reshape(n_kb, gpb * gc, gpb * n)

    def pack_c(cc):
        cc = cc.reshape(n_kb, gpb, gc, n)
        return jnp.einsum('kgcn,gh->kgnhc', cc, eye).reshape(n_kb, gpb * n, gpb * gc)

    wb = jnp.concatenate([pack_b(bb_re), pack_b(bb_im)], axis=2).astype(BF16)
    wc = jnp.concatenate([pack_c(c_re), pack_c(-c_im)], axis=1).astype(BF16)
    return ld_re.reshape(1, g * n), ld_im.reshape(1, g * n), wb, wc


def _pad_cols(w, n):
    return jnp.pad(w, ((0, 0), (0, n - w.shape[1])))


def _layer(x, h1, cfg, cos, sin, st, p, g_after):
    seq, bsz, sq, past = cfg["seq"], cfg["bsz"], cfg["sq"], cfg["past"]
    t_all, d = x.shape
    n_s = bsz * sq
    width = cfg["mix_w"]
    n_heads, c_heads = cfg["a_heads"], cfg["c_heads"]

    ff = p["ffn_w_down"].shape[1]
    ffp = _round_up(ff, 1024)

    def ffn(x, h, idx, g_post, g_next):
        w_up = jnp.concatenate([_pad_cols(p["ffn_w_up"][idx][:, :ff], ffp),
                                _pad_cols(p["ffn_w_up"][idx][:, ff:], ffp)], axis=1).astype(BF16)
        w_down = jnp.pad(p["ffn_w_down"][idx], ((0, ffp - ff), (0, 0))).astype(BF16)
        hid = swiglu_up(h, w_up, ffp)
        y = matmul(hid, w_down, out_dtype=F32, name="ffn_down")
        return add_rmsnorm(x, y, g_post, 0.5, g_next)

    g = p["norm_g"]
    x, h2 = ffn(x, h1, 0, g[1], g[2])

    a_cols = cfg["q_lora"] + cfg["kv_lora"] + cfg["a_rope"]
    cpad = _round_up(cfg["q_lora"] + cfg["kv_lora"] + LANE, width)
    c_cols = 3 * c_heads * LANE
    o_b, o_c, o_d, o_g = a_cols, a_cols + width, a_cols + width + c_cols, a_cols + 2 * width + c_cols
    w_in = p["w_in"]
    w_segf = jnp.concatenate([_pad_cols(w_in[:, :a_cols], cpad), w_in[:, o_b:o_c], w_in[:, o_d:o_g]],
                             axis=1).astype(BF16)
    segf = matmul(h2, w_segf, out_dtype=F32, name="w_in_f32")
    qkvc = matmul(h2, w_in[:, o_c:o_d].astype(BF16), out_dtype=F32, name="w_in_qkv")
    gates = matmul(h2, w_in[:, o_g:].astype(BF16), out_dtype=BF16, epilogue=_ep_sigmoid, name="w_in_gates")
    ub_col = cpad // width
    ud_col = ub_col + 1

    nope = LANE
    w_uq = p["mla_w_uq"].reshape(cfg["q_lora"], n_heads, nope + cfg["a_rope"])
    w_uq = jnp.pad(w_uq, ((0, 0), (0, 0), (0, 2 * LANE - nope - cfg["a_rope"])))
    w_uq = w_uq.reshape(cfg["q_lora"], n_heads * 2 * LANE).astype(BF16)
    w_ukv = p["mla_w_ukv"].astype(BF16)
    cq, ckv, ckv_b, kr, kr_b = mla_prep(segf, p["mla_q_norm"], p["mla_kv_norm"], cos, sin, cpad)
    tm_q = _tile(t_all, 1024)
    q = matmul(cq, w_uq, out_dtype=BF16, tm=tm_q, tn=1024, name="mla_q",
               epilogue=functools.partial(_ep_rope_q, scale=(nope + cfg["a_rope"]) ** -0.5 * math.log2(math.e),
                                          half=cfg["a_rope"] // 2),
               extra=[(cos, (tm_q, LANE), lambda i, j: (i, 0)), (sin, (tm_q, LANE), lambda i, j: (i, 0))])
    tm_kv = _tile(seq, 1024)
    kv_p = matmul(ckv_b, w_ukv, out_dtype=BF16, m=seq, tm=tm_kv, widen=2, name="mla_kv_p", epilogue=_ep_pack_kv,
                  extra=[(kr_b, (tm_kv, LANE), lambda i, j: (i, 0))])
    oa_p = mla_flash(q, kv_p, seq, n_heads)
    n_keys = past + sq
    kp = _round_up(n_keys, LANE)
    ckv_all = jnp.concatenate([st["ckv"].astype(BF16), ckv_b[seq:].reshape(bsz, sq, -1),
                               jnp.zeros((bsz, kp - n_keys, cfg["kv_lora"]), BF16)], axis=1)
    kr_all = jnp.concatenate([_pad_cols(st["kr"].reshape(bsz * past, -1), LANE).reshape(bsz, past, LANE).astype(BF16),
                              kr_b[seq:].reshape(bsz, sq, LANE), jnp.zeros((bsz, kp - n_keys, LANE), BF16)], axis=1)
    kv_s = matmul(ckv_all.reshape(bsz * kp, -1), w_ukv, out_dtype=BF16, name="mla_kv_s")
    oa_s = mla_sample(q, kv_s, kr_all.reshape(bsz * kp, LANE), seq, bsz, sq, n_heads, n_keys, past)
    o_a = jnp.concatenate([oa_p, oa_s], axis=0)

    lru = (p["lru_conv_w"], p["lru_conv_b"], p["lru_w_a"].astype(BF16), p["lru_b_a"],
           p["lru_w_x"].astype(BF16), p["lru_b_x"], p["lru_lambda"])
    n_conv = p["lru_conv_w"].shape[0] - 1
    ob_p, hl_p = rglru(segf, 0, ub_col, 1, seq, jnp.zeros((1, width), F32), jnp.zeros((1, n_conv, width), F32), *lru)
    ob_s, hl_s = rglru(segf, seq, ub_col, bsz, sq, st["h"], st["conv"], *lru)
    o_bb = jnp.concatenate([ob_p, ob_s], axis=0)
    conv_p = segf[seq - n_conv:seq, cpad:cpad + width][None]
    conv_s = segf[seq:, cpad:cpad + width].reshape(bsz, sq, width)[:, sq - n_conv:]

    win = BAND_CHUNKS * CHUNK
    tq_c = win
    assert seq % tq_c == 0
    bias_pp, bias_po = band_bias_tables(p["band_rel_bias"], tq_c, win, tq_c, tq_c)
    hps_p = 2 if c_heads % 2 == 0 else 1
    kcol, vcol = c_heads // hps_p, 2 * c_heads // hps_p
    prev_map = lambda col: (lambda i, g: (jnp.maximum(i - 1, 0), col + g))
    own_map = lambda col: (lambda i, g: (i, col + g))
    oc_p = band_attention(qkvc, 0, 0, qkvc, prev_map(kcol), qkvc, prev_map(vcol), qkvc, own_map(kcol),
                          qkvc, own_map(vcol), bias_pp, bias_po, n_heads=c_heads, hps=hps_p, n_blocks=seq // tq_c,
                          tq=tq_c, prev=win, own=tq_c, mask_first_prev=True)
    own_s = LANE
    k_new = qkvc[seq:, c_heads * LANE:2 * c_heads * LANE].reshape(bsz, sq, c_heads * LANE)
    v_new = qkvc[seq:, 2 * c_heads * LANE:].reshape(bsz, sq, c_heads * LANE)
    pad_own = lambda a: jnp.pad(a, ((0, 0), (0, own_s - sq), (0, 0))).reshape(bsz * own_s, c_heads * LANE)
    bias_sp, bias_so = band_bias_tables(p["band_rel_bias"], sq, win, own_s, sq)
    cache_k, cache_v, cache_blk0 = st["bk_all"], st["bv_all"], st["layer"] * bsz
    batch_map = lambda i, h: (i, h)
    cache_map = lambda i, h: (cache_blk0 + i, h)
    oc_s = band_attention(qkvc, seq, 0, cache_k, cache_map, cache_v, cache_map, pad_own(k_new), batch_map,
                          pad_own(v_new), batch_map, bias_sp, bias_so, n_heads=c_heads, hps=c_heads, n_blocks=bsz,
                          tq=sq, prev=win, own=own_s, mask_first_prev=False)
    o_cc = jnp.concatenate([oc_p, oc_s], axis=0)
    keep = min(win, seq)
    bk_p = qkvc[seq - keep:seq, c_heads * LANE:2 * c_heads * LANE].reshape(1, keep, c_heads, LANE)
    bv_p = qkvc[seq - keep:seq, 2 * c_heads * LANE:].reshape(1, keep, c_heads, LANE)
    bk_s = k_new.reshape(bsz, sq, c_heads, LANE)
    bv_s = v_new.reshape(bsz, sq, c_heads, LANE)

    n_grp, n_state = p["s5_a_re"].shape
    abar_re, abar_im, wb, wc = s5_params(p["s5_a_re"], p["s5_a_im"], p["s5_log_dt"], p["s5_b_re"], p["s5_b_im"],
                                         p["s5_c_re"], p["s5_c_im"], LANE)
    pack0 = lambda re, im: jnp.concatenate([re.reshape(-1, 1, n_grp * n_state), im.reshape(-1, 1, n_grp * n_state)],
                                           axis=2)
    zeros0 = jnp.zeros((1, n_grp, n_state), F32)
    z_p, xl_p = s5_scan(segf, 0, ud_col, 1, seq, pack0(zeros0, zeros0), abar_re, abar_im, wb, wc, p["s5_d"])
    z_s, xl_s = s5_scan(segf, seq, ud_col, bsz, sq, pack0(st["s5re"], st["s5im"]), abar_re, abar_im, wb, wc,
                        p["s5_d"])
    z = jnp.concatenate([z_p, z_s], axis=0)
    tm_g = _tile(t_all, 1024)
    tn_g = _tile(width, 1024)
    o_dd = matmul(z.astype(BF16), p["s5_w_glu"].astype(BF16), out_dtype=BF16, tm=tm_g, tn=tn_g, epilogue=_ep_glu,
                  extra=[(z, (tm_g, tn_g), lambda i, j: (i, j)),
                         (p["s5_b_glu"].reshape(1, -1), (1, tn_g), lambda i, j: (0, j))], name="s5_glu")
    unpack = lambda xl, part: xl[:, part * n_grp * n_state:(part + 1) * n_grp * n_state].reshape(-1, n_grp, n_state)

    merged = merge_branches((o_a, o_bb, o_cc, o_dd), p["w_branch"].astype(BF16), gates)
    y = matmul(merged, p["w_out"].astype(BF16), out_dtype=F32, name="w_out")
    x, h3 = add_rmsnorm(x, y, g[3], 1.0, g[4])
    if g_after is None:
        x, h_after = ffn(x, h3, 1, g[5], None), None
    else:
        x, h_after = ffn(x, h3, 1, g[5], g_after)

    new_p = (ckv[:seq][None], kr[:seq, :cfg["a_rope"]][None], hl_p, conv_p, bk_p, bv_p, unpack(xl_p, 0),
             unpack(xl_p, 1))
    new_s = (ckv[seq:].reshape(bsz, sq, -1), kr[seq:, :cfg["a_rope"]].reshape(bsz, sq, -1), hl_s, conv_s, bk_s, bv_s,
             unpack(xl_s, 0), unpack(xl_s, 1))
    return x, h_after, new_p, new_s


def kernel(x_prompt, x_sample, cache_mla_ckv, cache_mla_krope, state_lru_h, state_lru_conv, cache_band_k, cache_band_v, state_s5_re, state_s5_im, norm_g, ffn_w_up, ffn_w_down, w_in, mla_q_norm, mla_kv_norm, mla_w_uq, mla_w_ukv, lru_conv_w, lru_conv_b, lru_w_a, lru_b_a, lru_w_x, lru_b_x, lru_lambda, band_rel_bias, s5_a_re, s5_a_im, s5_log_dt, s5_b_re, s5_b_im, s5_c_re, s5_c_im, s5_d, s5_w_glu, s5_b_glu, w_branch, w_out):
    depth = norm_g.shape[0]
    n_p, seq, d = x_prompt.shape
    bsz, sq, _ = x_sample.shape
    past = cache_mla_ckv.shape[2]
    a_rope = cache_mla_krope.shape[3]
    mix_w = w_branch.shape[2]
    a_heads = mix_w // LANE
    c_heads = cache_band_k.shape[3]
    assert n_p == 1 and a_rope == 2 * ROPE_HALF and cache_band_k.shape[4] == LANE
    assert mla_w_uq.shape[2] == a_heads * (LANE + a_rope) and mla_w_ukv.shape[2] == a_heads * 2 * LANE
    assert past % CHUNK == 0 and cache_band_k.shape[2] == BAND_CHUNKS * CHUNK and past >= BAND_CHUNKS * CHUNK
    assert sq <= CHUNK and seq % (BAND_CHUNKS * CHUNK) == 0 and sq >= lru_conv_w.shape[1] - 1
    cfg = dict(seq=seq, bsz=bsz, sq=sq, past=past, mix_w=mix_w, a_heads=a_heads, c_heads=c_heads,
               q_lora=mla_q_norm.shape[1], kv_lora=mla_kv_norm.shape[1], a_rope=a_rope)

    pos = np.concatenate([np.arange(seq), np.tile(past + np.arange(sq), bsz)]).astype(np.float32)
    half = a_rope // 2
    inv = ROPE_THETA ** (-jnp.arange(half, dtype=F32) / half)
    ang = jnp.asarray(pos)[:, None] * inv
    zpad = jnp.zeros((pos.shape[0], LANE - a_rope), F32)
    cos = jnp.concatenate([jnp.cos(ang), jnp.cos(ang), zpad], axis=1)
    sin = jnp.concatenate([-jnp.sin(ang), jnp.sin(ang), zpad], axis=1)

    x = jnp.concatenate([x_prompt.reshape(seq, d), x_sample.reshape(bsz * sq, d)], axis=0)
    h = rmsnorm(x, norm_g[0, 0])
    new_p, new_s = [], []
    for l in range(depth):
        st = dict(ckv=cache_mla_ckv[l], kr=cache_mla_krope[l], h=state_lru_h[l], conv=state_lru_conv[l],
                  bk_all=cache_band_k.reshape(-1, c_heads * LANE), bv_all=cache_band_v.reshape(-1, c_heads * LANE),
                  layer=l, s5re=state_s5_re[l], s5im=state_s5_im[l])
        p = dict(norm_g=norm_g[l], ffn_w_up=ffn_w_up[l], ffn_w_down=ffn_w_down[l], w_in=w_in[l],
                 mla_q_norm=mla_q_norm[l], mla_kv_norm=mla_kv_norm[l], mla_w_uq=mla_w_uq[l], mla_w_ukv=mla_w_ukv[l],
                 lru_conv_w=lru_conv_w[l], lru_conv_b=lru_conv_b[l], lru_w_a=lru_w_a[l], lru_b_a=lru_b_a[l],
                 lru_w_x=lru_w_x[l], lru_b_x=lru_b_x[l], lru_lambda=lru_lambda[l], band_rel_bias=band_rel_bias[l],
                 s5_a_re=s5_a_re[l], s5_a_im=s5_a_im[l], s5_log_dt=s5_log_dt[l], s5_b_re=s5_b_re[l],
                 s5_b_im=s5_b_im[l], s5_c_re=s5_c_re[l], s5_c_im=s5_c_im[l], s5_d=s5_d[l], s5_w_glu=s5_w_glu[l],
                 s5_b_glu=s5_b_glu[l], w_branch=w_branch[l], w_out=w_out[l])
        x, h, st_p, st_s = _layer(x, h, cfg, cos, sin, st, p, norm_g[l + 1, 0] if l + 1 < depth else None)
        new_p.append(st_p)
        new_s.append(st_s)

    outs = [x[:seq].reshape(1, seq, d), x[seq:].reshape(bsz, sq, d)]
    for i in range(8):
        outs.append(jnp.stack([s[i] for s in new_p], axis=0))
        outs.append(jnp.stack([s[i] for s in new_s], axis=0))
    return tuple(outs)
```

```python
import functools
import math

import jax
import jax.numpy as jnp
import numpy as np
from jax import lax
from jax.experimental import pallas as pl
from jax.experimental.pallas import tpu as pltpu

F32 = jnp.float32
BF16 = jnp.bfloat16

CHUNK = 64
BAND_CHUNKS = 8
EPS = 1e-6
ROPE_THETA = 10000.0
LRU_C = 8.0
ROPE_HALF = 32
MASK_VALUE = -1e30
LANE = 128
SUBLANE = 8
VMEM_LIMIT = 56 * 1024 * 1024


def _cparams(*sem):
    return pltpu.CompilerParams(dimension_semantics=sem, vmem_limit_bytes=VMEM_LIMIT)


def _round_up(x, m):
    return (x + m - 1) // m * m


def _tile(dim, pref):
    if dim <= pref:
        return dim
    t = pref
    while dim % t:
        t //= 2
    assert t >= SUBLANE, (dim, pref)
    return t


def _rmsnorm_kernel(x_ref, g_ref, o_ref):
    x = x_ref[...]
    ms = jnp.mean(x * x, axis=-1, keepdims=True)
    o_ref[...] = (x * lax.rsqrt(ms + EPS) * g_ref[...]).astype(o_ref.dtype)


def rmsnorm(x, g, out_dtype=BF16):
    t, d = x.shape
    tm = _tile(t, 256)
    return pl.pallas_call(
        _rmsnorm_kernel,
        grid=(t // tm,),
        in_specs=[pl.BlockSpec((tm, d), lambda i: (i, 0)), pl.BlockSpec((1, d), lambda i: (0, 0))],
        out_specs=pl.BlockSpec((tm, d), lambda i: (i, 0)),
        out_shape=jax.ShapeDtypeStruct((t, d), out_dtype),
        compiler_params=_cparams("parallel"),
        name="rmsnorm",
    )(x, g.reshape(1, d))


def _add_rmsnorm_kernel(x_ref, y_ref, g_ref, *rest, scale, with_next):
    y = y_ref[...]
    ms = jnp.mean(y * y, axis=-1, keepdims=True)
    x = x_ref[...] + scale * (y * lax.rsqrt(ms + EPS) * g_ref[...])
    if not with_next:
        rest[0][...] = x
        return
    gn_ref, o_ref, h_ref = rest
    o_ref[...] = x
    ms = jnp.mean(x * x, axis=-1, keepdims=True)
    h_ref[...] = (x * lax.rsqrt(ms + EPS) * gn_ref[...]).astype(h_ref.dtype)


def add_rmsnorm(x, y, g, scale, g_next=None):
    t, d = x.shape
    tm = _tile(t, 256)
    row = pl.BlockSpec((tm, d), lambda i: (i, 0))
    vec = pl.BlockSpec((1, d), lambda i: (0, 0))
    with_next = g_next is not None
    return pl.pallas_call(
        functools.partial(_add_rmsnorm_kernel, scale=scale, with_next=with_next),
        grid=(t // tm,),
        in_specs=[row, row, vec] + ([vec] if with_next else []),
        out_specs=[row, row] if with_next else row,
        out_shape=([jax.ShapeDtypeStruct((t, d), F32), jax.ShapeDtypeStruct((t, d), BF16)] if with_next
                   else jax.ShapeDtypeStruct((t, d), F32)),
        compiler_params=_cparams("parallel"),
        name="add_rmsnorm",
    )(x, y, g.reshape(1, d), *([g_next.reshape(1, d)] if with_next else []))


def _k_tile(k):
    if k <= 4096:
        return k
    for parts in range(2, k // LANE + 1):
        if k % parts == 0 and (k // parts) % LANE == 0 and k // parts <= 4096:
            return k // parts
    raise ValueError(k)


def _mm_kernel(*refs, nk, n_extra, epilogue):
    a_ref, b_ref = refs[0], refs[1]
    extra = refs[2:2 + n_extra]
    o_ref = refs[2 + n_extra]
    if nk == 1:
        acc = jnp.dot(a_ref[...], b_ref[...], preferred_element_type=F32)
        o_ref[...] = epilogue(acc, *extra).astype(o_ref.dtype)
        return
    acc_ref = refs[3 + n_extra]
    k = pl.program_id(2)

    @pl.when(k == 0)
    def _():
        acc_ref[...] = jnp.zeros_like(acc_ref)

    acc_ref[...] += jnp.dot(a_ref[...], b_ref[...], preferred_element_type=F32)

    @pl.when(k == nk - 1)
    def _():
        o_ref[...] = epilogue(acc_ref[...], *extra).astype(o_ref.dtype)


def _ep_none(acc):
    return acc


def _ep_sigmoid(acc):
    return jax.nn.sigmoid(acc)


def _ep_glu(acc, z_ref, b_ref):
    z = z_ref[...]
    return z * jax.nn.sigmoid(acc + b_ref[...])


def _ep_pack_kv(acc, kr_ref):
    kr = kr_ref[...].astype(F32)
    ones = jnp.ones_like(kr)
    outs = []
    for h in range(acc.shape[1] // (2 * LANE)):
        outs += [acc[:, h * 2 * LANE:h * 2 * LANE + LANE], kr, acc[:, h * 2 * LANE + LANE:(h + 1) * 2 * LANE], ones]
    return jnp.concatenate(outs, axis=1)


def _ep_rope_q(acc, cos_ref, sin_ref, *, scale, half):
    cos, sin = cos_ref[...], sin_ref[...]
    lane = lax.broadcasted_iota(jnp.int32, cos.shape, 1)
    outs = []
    for h in range(acc.shape[1] // (2 * LANE)):
        nope = acc[:, h * 2 * LANE:h * 2 * LANE + LANE]
        r = acc[:, h * 2 * LANE + LANE:(h + 1) * 2 * LANE]
        swapped = jnp.where(lane < half, pltpu.roll(r, LANE - half, axis=1), pltpu.roll(r, half, axis=1))
        outs += [nope * scale, (r * cos + swapped * sin) * scale]
    return jnp.concatenate(outs, axis=1)


def matmul(a, b, *, out_dtype, m=None, tm=1024, tn=1024, tk=None, b_sel=(), epilogue=_ep_none, extra=(), widen=1,
           name="matmul"):
    m = a.shape[0] if m is None else m
    k, n = b.shape[-2:]
    assert a.shape[1] == k and b.ndim == 2 + len(b_sel)
    tm, tn, tk = _tile(m, tm), _tile(n, tn), (_k_tile(k) if tk is None else tk)
    nk = k // tk
    lead = (None,) * len(b_sel)
    if nk == 1:
        grid = (m // tm, n // tn)
        wrap = lambda f: f
        a_spec = pl.BlockSpec((tm, k), lambda i, j: (i, 0))
        b_spec = pl.BlockSpec(lead + (k, tn), lambda i, j: (*b_sel, 0, j))
        o_spec = pl.BlockSpec((tm, tn * widen), lambda i, j: (i, j))
        scratch = []
        sem = ("parallel", "parallel")
    else:
        grid = (m // tm, n // tn, nk)
        wrap = lambda f: (lambda i, j, kk: f(i, j))
        a_spec = pl.BlockSpec((tm, tk), lambda i, j, kk: (i, kk))
        b_spec = pl.BlockSpec(lead + (tk, tn), lambda i, j, kk: (*b_sel, kk, j))
        o_spec = pl.BlockSpec((tm, tn * widen), lambda i, j, kk: (i, j))
        scratch = [pltpu.VMEM((tm, tn), F32)]
        sem = ("parallel", "parallel", "arbitrary")
    extra_specs = [pl.BlockSpec(bs, wrap(im)) for _, bs, im in extra]
    return pl.pallas_call(
        functools.partial(_mm_kernel, nk=nk, n_extra=len(extra), epilogue=epilogue),
        grid=grid,
        in_specs=[a_spec, b_spec] + extra_specs,
        out_specs=o_spec,
        out_shape=jax.ShapeDtypeStruct((m, n * widen), out_dtype),
        scratch_shapes=scratch,
        compiler_params=_cparams(*sem),
        name=name,
    )(a, b, *[e[0] for e in extra])


def _swiglu_kernel(a_ref, wg_ref, wu_ref, o_ref):
    a = a_ref[...]
    gate = jnp.dot(a, wg_ref[...], preferred_element_type=F32)
    up = jnp.dot(a, wu_ref[...], preferred_element_type=F32)
    o_ref[...] = (jax.nn.silu(gate) * up).astype(o_ref.dtype)


def swiglu_up(a, w_up, sel):
    t, d = a.shape
    ff = w_up.shape[-1] // 2
    tm, tn = _tile(t, 1024), _tile(ff, 512)
    nj = ff // tn
    lead = (None,) * len(sel)
    return pl.pallas_call(
        _swiglu_kernel,
        grid=(t // tm, nj),
        in_specs=[pl.BlockSpec((tm, d), lambda i, j: (i, 0)),
                  pl.BlockSpec(lead + (d, tn), lambda i, j: (*sel, 0, j)),
                  pl.BlockSpec(lead + (d, tn), lambda i, j: (*sel, 0, j + nj))],
        out_specs=pl.BlockSpec((tm, tn), lambda i, j: (i, j)),
        out_shape=jax.ShapeDtypeStruct((t, ff), BF16),
        compiler_params=_cparams("parallel", "parallel"),
        name="swiglu_up",
    )(a, w_up, w_up)


def _merge_kernel(oa_ref, ob_ref, oc_ref, od_ref, wb_ref, ga_ref, gb_ref, gc_ref, gd_ref, o_ref):
    total = None
    for i, (o_r, g_r) in enumerate(((oa_ref, ga_ref), (ob_ref, gb_ref), (oc_ref, gc_ref), (od_ref, gd_ref))):
        part = g_r[...].astype(F32) * jnp.dot(o_r[...], wb_ref[i], preferred_element_type=F32)
        total = part if total is None else total + part
    o_ref[...] = total.astype(o_ref.dtype)


def merge_branches(outs, w_branch, layer, gates):
    t, w = outs[0].shape
    d = w_branch.shape[3]
    tm, tn = _tile(t, 512), _tile(d, 512)
    nj = d // tn
    o_spec = pl.BlockSpec((tm, w), lambda i, j: (i, 0))
    g_specs = [pl.BlockSpec((tm, tn), functools.partial(lambda i, j, b: (i, j + b * nj), b=b)) for b in range(4)]
    return pl.pallas_call(
        _merge_kernel,
        grid=(t // tm, nj),
        in_specs=[o_spec] * 4 + [pl.BlockSpec((None, 4, w, tn), lambda i, j: (layer, 0, 0, j))] + g_specs,
        out_specs=pl.BlockSpec((tm, tn), lambda i, j: (i, j)),
        out_shape=jax.ShapeDtypeStruct((t, d), BF16),
        compiler_params=_cparams("parallel", "parallel"),
        name="merge_branches",
    )(*outs, w_branch, gates, gates, gates, gates)


def _mla_prep_kernel(c_ref, qn_ref, kvn_ref, cos_ref, sin_ref, cq_ref, ckv_ref, ckvb_ref, kr_ref, krb_ref,
                     *, q_lora, kv_lora, half):
    c_q = c_ref[:, :q_lora]
    ms = jnp.mean(c_q * c_q, axis=-1, keepdims=True)
    cq_ref[...] = (c_q * lax.rsqrt(ms + EPS) * qn_ref[...]).astype(cq_ref.dtype)
    c_kv = c_ref[:, q_lora:q_lora + kv_lora]
    ms = jnp.mean(c_kv * c_kv, axis=-1, keepdims=True)
    ckv = c_kv * lax.rsqrt(ms + EPS) * kvn_ref[...]
    ckv_ref[...] = ckv
    ckvb_ref[...] = ckv.astype(ckvb_ref.dtype)
    r = c_ref[:, q_lora + kv_lora:q_lora + kv_lora + LANE]
    lane = lax.broadcasted_iota(jnp.int32, r.shape, 1)
    swapped = jnp.where(lane < half, pltpu.roll(r, LANE - half, axis=1), pltpu.roll(r, half, axis=1))
    kr = r * cos_ref[...] + swapped * sin_ref[...]
    kr_ref[...] = kr
    krb_ref[...] = kr.astype(krb_ref.dtype)


def mla_prep(segf, q_norm, kv_norm, cos, sin, cpad):
    t = segf.shape[0]
    q_lora, kv_lora = q_norm.shape[0], kv_norm.shape[0]
    tm = _tile(t, 256)
    row = lambda w: pl.BlockSpec((tm, w), lambda i: (i, 0))
    const = lambda w: pl.BlockSpec((1, w), lambda i: (0, 0))
    return pl.pallas_call(
        functools.partial(_mla_prep_kernel, q_lora=q_lora, kv_lora=kv_lora, half=ROPE_HALF),
        grid=(t // tm,),
        in_specs=[row(cpad), const(q_lora), const(kv_lora), row(LANE), row(LANE)],
        out_specs=[row(q_lora), row(kv_lora), row(kv_lora), row(LANE), row(LANE)],
        out_shape=[jax.ShapeDtypeStruct((t, q_lora), BF16), jax.ShapeDtypeStruct((t, kv_lora), F32),
                   jax.ShapeDtypeStruct((t, kv_lora), BF16), jax.ShapeDtypeStruct((t, LANE), F32),
                   jax.ShapeDtypeStruct((t, LANE), BF16)],
        compiler_params=_cparams("parallel"),
        name="mla_prep",
    )(segf, q_norm.reshape(1, -1), kv_norm.reshape(1, -1), cos, sin)


def _mla_flash_kernel(qi_ref, ki_ref, q_ref, kv_ref, o_ref, m_ref, acc_ref, *, heads, tq, tk):
    s_id = pl.program_id(1)
    qi, ki = qi_ref[s_id], ki_ref[s_id]
    hw = 2 * LANE

    @pl.when(ki == 0)
    def _():
        m_ref[...] = jnp.full_like(m_ref, -jnp.inf)
        acc_ref[...] = jnp.zeros_like(acc_ref)

    def step(masked):
        if masked:
            rc = lax.broadcasted_iota(jnp.int32, (tq, tk), 0) // CHUNK
            cc = lax.broadcasted_iota(jnp.int32, (tq, tk), 1) // CHUNK
            visible = cc <= rc
        for h in range(heads):
            q = q_ref[:, h * hw:(h + 1) * hw]
            k = kv_ref[:, 2 * h * hw:(2 * h + 1) * hw]
            v1 = kv_ref[:, (2 * h + 1) * hw:(2 * h + 2) * hw]
            s = lax.dot_general(q, k, (((1,), (1,)), ((), ())), preferred_element_type=F32)
            if masked:
                s = jnp.where(visible, s, MASK_VALUE)
            m_prev = m_ref[h]
            m_next = jnp.maximum(m_prev, jnp.max(s, axis=1)[:, None])
            p = jnp.exp2(s - jnp.tile(m_next, (1, tk // LANE)))
            alpha = jnp.exp2(m_prev - m_next)
            m_ref[h] = m_next
            pv = jnp.dot(p.astype(BF16), v1, preferred_element_type=F32)
            acc_ref[:, h * hw:(h + 1) * hw] = jnp.tile(alpha, (1, 2)) * acc_ref[:, h * hw:(h + 1) * hw] + pv

    @pl.when(ki < qi)
    def _():
        step(False)

    @pl.when(ki == qi)
    def _():
        step(True)
        for h in range(heads):
            o_ref[:, h * LANE:(h + 1) * LANE] = (
                acc_ref[:, h * hw:h * hw + LANE] / acc_ref[:, h * hw + LANE:(h + 1) * hw]).astype(o_ref.dtype)


def mla_flash(q, kv, seq, n_heads, *, tile=1024, heads_per_step=4):
    tq = tk = _tile(seq, tile)
    hg = min(heads_per_step, n_heads)
    assert n_heads % hg == 0 and tq % CHUNK == 0
    nq = seq // tq
    pairs = [(i, j) for i in range(nq) for j in range(i + 1)]
    qi_arr = jnp.asarray(np.array([p[0] for p in pairs], np.int32))
    ki_arr = jnp.asarray(np.array([p[1] for p in pairs], np.int32))
    grid_spec = pltpu.PrefetchScalarGridSpec(
        num_scalar_prefetch=2,
        grid=(n_heads // hg, len(pairs)),
        in_specs=[pl.BlockSpec((tq, hg * 2 * LANE), lambda g, s, qi, ki: (qi[s], g)),
                  pl.BlockSpec((tk, hg * 4 * LANE), lambda g, s, qi, ki: (ki[s], g))],
        out_specs=pl.BlockSpec((tq, hg * LANE), lambda g, s, qi, ki: (qi[s], g)),
        scratch_shapes=[pltpu.VMEM((hg, tq, LANE), F32), pltpu.VMEM((tq, hg * 2 * LANE), F32)],
    )
    return pl.pallas_call(
        functools.partial(_mla_flash_kernel, heads=hg, tq=tq, tk=tk),
        grid_spec=grid_spec,
        out_shape=jax.ShapeDtypeStruct((seq, n_heads * LANE), BF16),
        compiler_params=_cparams("parallel", "arbitrary"),
        name="mla_flash",
    )(qi_arr, ki_arr, q, kv)


def _mla_sample_kernel(q_ref, kv_ref, kr_ref, o_ref, *, heads, n_keys, past):
    sq, kp = q_ref.shape[0], kv_ref.shape[0]
    kr = kr_ref[...]
    qpos = past + lax.broadcasted_iota(jnp.int32, (sq, kp), 0)
    kpos = lax.broadcasted_iota(jnp.int32, (sq, kp), 1)
    visible = (kpos < n_keys) & (kpos // CHUNK <= qpos // CHUNK)
    for h in range(heads):
        q = q_ref[:, h * 2 * LANE:(h + 1) * 2 * LANE]
        k = jnp.concatenate([kv_ref[:, h * 2 * LANE:h * 2 * LANE + LANE], kr], axis=1)
        v = kv_ref[:, h * 2 * LANE + LANE:(h + 1) * 2 * LANE]
        s = lax.dot_general(q, k, (((1,), (1,)), ((), ())), preferred_element_type=F32)
        s = jnp.where(visible, s, MASK_VALUE)
        p = jnp.exp2(s - jnp.max(s, axis=-1, keepdims=True))
        denom = jnp.sum(p, axis=-1, keepdims=True)
        pv = jnp.dot(p.astype(v.dtype), v, preferred_element_type=F32)
        o_ref[:, h * LANE:(h + 1) * LANE] = (pv / denom).astype(o_ref.dtype)


def mla_sample(q, kv, kr, row0, bsz, sq, n_heads, n_keys, past):
    kp = kv.shape[0] // bsz
    assert row0 % sq == 0
    return pl.pallas_call(
        functools.partial(_mla_sample_kernel, heads=n_heads, n_keys=n_keys, past=past),
        grid=(bsz,),
        in_specs=[pl.BlockSpec((sq, n_heads * 2 * LANE), lambda b: (row0 // sq + b, 0)),
                  pl.BlockSpec((kp, n_heads * 2 * LANE), lambda b: (b, 0)),
                  pl.BlockSpec((kp, LANE), lambda b: (b, 0))],
        out_specs=pl.BlockSpec((sq, n_heads * LANE), lambda b: (b, 0)),
        out_shape=jax.ShapeDtypeStruct((bsz * sq, n_heads * LANE), BF16),
        compiler_params=_cparams("parallel"),
        name="mla_sample",
    )(q, kv, kr)


def _band_kernel(q_ref, kp_ref, ko_ref, vp_ref, vo_ref, bp_ref, bo_ref, o_ref, *, heads, scale, mask_first_prev):
    dn = (((1,), (1,)), ((), ()))
    ones_p = jnp.ones((kp_ref.shape[0], LANE), BF16)
    ones_o = jnp.ones((ko_ref.shape[0], LANE), BF16)
    for h in range(heads):
        cs = slice(h * LANE, (h + 1) * LANE)
        q = (q_ref[:, cs] * scale).astype(BF16)
        sp = lax.dot_general(q, kp_ref[:, cs].astype(BF16), dn, preferred_element_type=F32) + bp_ref[h]
        so = lax.dot_general(q, ko_ref[:, cs].astype(BF16), dn, preferred_element_type=F32) + bo_ref[h]
        if mask_first_prev:
            sp = jnp.where(pl.program_id(1) > 0, sp, MASK_VALUE)
        m = jnp.maximum(jnp.max(sp, axis=-1, keepdims=True), jnp.max(so, axis=-1, keepdims=True))
        pp, po = jnp.exp2(sp - m), jnp.exp2(so - m)
        vp1 = jnp.concatenate([vp_ref[:, cs].astype(BF16), ones_p], axis=1)
        vo1 = jnp.concatenate([vo_ref[:, cs].astype(BF16), ones_o], axis=1)
        pv = (jnp.dot(pp.astype(BF16), vp1, preferred_element_type=F32)
              + jnp.dot(po.astype(BF16), vo1, preferred_element_type=F32))
        o_ref[:, cs] = (pv[:, :LANE] / pv[:, LANE:]).astype(o_ref.dtype)


def band_attention(q_arr, q_row0, q_col0, kp_arr, kp_map, vp_arr, vp_map, ko_arr, ko_map, vo_arr, vo_map,
                   bias_p, bias_o, *, n_heads, hps, n_blocks, tq, prev, own, mask_first_prev):
    hd = LANE
    w = hps * hd
    assert q_row0 % tq == 0 and n_heads % hps == 0
    return pl.pallas_call(
        functools.partial(_band_kernel, heads=hps, scale=hd ** -0.5 * math.log2(math.e),
                          mask_first_prev=mask_first_prev),
        grid=(n_heads // hps, n_blocks),
        in_specs=[pl.BlockSpec((tq, w), lambda g, i: (q_row0 // tq + i, q_col0 + g)),
                  pl.BlockSpec((prev, w), lambda g, i: kp_map(i, g)),
                  pl.BlockSpec((own, w), lambda g, i: ko_map(i, g)),
                  pl.BlockSpec((prev, w), lambda g, i: vp_map(i, g)),
                  pl.BlockSpec((own, w), lambda g, i: vo_map(i, g)),
                  pl.BlockSpec((hps, tq, prev), lambda g, i: (g, 0, 0)),
                  pl.BlockSpec((hps, tq, own), lambda g, i: (g, 0, 0))],
        out_specs=pl.BlockSpec((tq, w), lambda g, i: (i, g)),
        out_shape=jax.ShapeDtypeStruct((n_blocks * tq, n_heads * hd), BF16),
        compiler_params=_cparams("parallel", "arbitrary"),
        name="band_attention",
    )(q_arr, kp_arr, ko_arr, vp_arr, vo_arr, bias_p, bias_o)


def band_bias_tables(rel_bias, tq, prev, own, own_valid):
    n_heads = rel_bias.shape[0]
    clip = (rel_bias.shape[1] - 1) // 2
    qr = np.arange(tq)[:, None]

    def table(krel, valid):
        nk, k0 = krel.shape[1], int(krel[0, 0])
        period = nk + tq
        e = np.zeros(period, np.int64)
        e[:nk] = -np.arange(nk)
        e[nk + 1:] = np.arange(tq - 1, 0, -1)
        per_offset = jnp.take(rel_bias, jnp.asarray(np.clip(e - k0, -clip, clip) + clip), axis=1)
        toeplitz = jnp.tile(per_offset, (1, tq))[:, :tq * (period - 1)].reshape(n_heads, tq, period - 1)[:, :, :nk]
        dc = qr // CHUNK - np.floor_divide(krel, CHUNK)
        vis = (dc >= 0) & (dc <= BAND_CHUNKS) & valid
        return jnp.where(jnp.asarray(vis)[None], toeplitz * math.log2(math.e), MASK_VALUE)

    kp = np.arange(-prev, 0)[None, :]
    ko = np.arange(own)[None, :]
    return table(kp, np.ones_like(kp, bool)), table(ko, ko < own_valid)


def _rglru_kernel(u_ref, h0_ref, c0_ref, cw_ref, cb_ref, wa_ref, ba_ref, wx_ref, bx_ref, lam_ref,
                  o_ref, hl_ref, prev_ref, h_ref, a_ref, b_ref, *, n_blocks, bw):
    t = pl.program_id(1)
    tb, width = u_ref.shape

    @pl.when(t == 0)
    def _():
        prev_ref[...] = c0_ref[0]
        h_ref[...] = h0_ref[0]

    u = u_ref[...]
    cw = cw_ref[...]
    nw = 4
    xc = cb_ref[...] + cw[nw - 1:nw, :] * u
    for s in range(1, nw):
        xc = xc + cw[nw - 1 - s:nw - s, :] * pltpu.roll(u, s, axis=0)
    head = jnp.concatenate([prev_ref[...], u[:SUBLANE, :]], axis=0)
    xh = cb_ref[...] + cw[nw - 1:nw, :] * u[:SUBLANE, :]
    for s in range(1, nw):
        xh = xh + cw[nw - 1 - s:nw - s, :] * pltpu.roll(head, s, axis=0)[SUBLANE:, :]
    a_ref[:SUBLANE, :] = xh
    a_ref[SUBLANE:, :] = xc[SUBLANE:, :]
    xc = a_ref[...]
    prev_ref[...] = u[tb - SUBLANE:, :]

    neg_c_sp = -LRU_C * jax.nn.softplus(-lam_ref[...])
    for n in range(n_blocks):
        sl = slice(n * bw, (n + 1) * bw)
        xb = xc[:, sl].astype(BF16)
        r = jax.nn.sigmoid(jnp.dot(xb, wa_ref[n], preferred_element_type=F32) + ba_ref[:, sl])
        i = jax.nn.sigmoid(jnp.dot(xb, wx_ref[n], preferred_element_type=F32) + bx_ref[:, sl])
        log_a = neg_c_sp[:, sl] * r
        a = jnp.exp(log_a)
        a_ref[:, sl] = a
        b_ref[:, sl] = jnp.sqrt(-jnp.tanh(log_a) * (a * a + 1.0)) * (i * xc[:, sl])

    def row(j, h):
        h = a_ref[pl.ds(j, 1), :] * h + b_ref[pl.ds(j, 1), :]
        b_ref[pl.ds(j, 1), :] = h
        return h

    h = lax.fori_loop(0, tb, row, h_ref[...])
    h_ref[...] = h
    o_ref[...] = b_ref[...].astype(o_ref.dtype)
    hl_ref[0] = h


def rglru(u_arr, row0, col0, n_seq, seq, h0, conv0, conv_w, conv_b, w_a, b_a, w_x, b_x, lam):
    width = h0.shape[-1]
    n_blocks, bw = w_a.shape[0], w_a.shape[1]
    tb = _tile(seq, 256)
    nt = seq // tb
    assert row0 % tb == 0 and tb >= 2 * SUBLANE and conv_w.shape[0] == 4
    conv0p = jnp.concatenate([jnp.zeros((n_seq, SUBLANE - conv0.shape[1], width), F32), conv0], axis=1)
    cwp = jnp.concatenate([conv_w, jnp.zeros((SUBLANE - conv_w.shape[0], width), F32)], axis=0)
    vec = lambda: pl.BlockSpec((1, width), lambda s, t: (0, 0))
    wspec = lambda: pl.BlockSpec((n_blocks, bw, bw), lambda s, t: (0, 0, 0))
    out, hl = pl.pallas_call(
        functools.partial(_rglru_kernel, n_blocks=n_blocks, bw=bw),
        grid=(n_seq, nt),
        in_specs=[pl.BlockSpec((tb, width), lambda s, t: (row0 // tb + s * nt + t, col0)),
                  pl.BlockSpec((1, 1, width), lambda s, t: (s, 0, 0)),
                  pl.BlockSpec((1, SUBLANE, width), lambda s, t: (s, 0, 0)),
                  pl.BlockSpec((SUBLANE, width), lambda s, t: (0, 0)),
                  vec(), wspec(), vec(), wspec(), vec(), vec()],
        out_specs=[pl.BlockSpec((tb, width), lambda s, t: (s * nt + t, 0)),
                   pl.BlockSpec((1, 1, width), lambda s, t: (s, 0, 0))],
        out_shape=[jax.ShapeDtypeStruct((n_seq * seq, width), BF16),
                   jax.ShapeDtypeStruct((n_seq, 1, width), F32)],
        scratch_shapes=[pltpu.VMEM((SUBLANE, width), F32), pltpu.VMEM((1, width), F32),
                        pltpu.VMEM((tb, width), F32), pltpu.VMEM((tb, width), F32)],
        compiler_params=_cparams("parallel", "arbitrary"),
        name="rglru",
    )(u_arr, h0.reshape(n_seq, 1, width), conv0p, cwp, conv_b.reshape(1, -1), w_a, b_a.reshape(1, -1),
      w_x, b_x.reshape(1, -1), lam.reshape(1, -1))
    return out, hl.reshape(n_seq, width)


def _permute_rows(perm, x):
    hi = x.astype(BF16)
    rest = x - hi.astype(F32)
    mid = rest.astype(BF16)
    lo = (rest - mid.astype(F32)).astype(BF16)
    dot = lambda piece: jnp.dot(perm, piece, preferred_element_type=F32)
    return (dot(hi) + dot(mid)) + dot(lo)


def _s5_kernel(u_ref, x0_ref, pre_ref, pim_ref, wb_ref, wc_ref, d_ref, z_ref, xl_ref, up_ref, xs_ref, xc_ref,
               *, n_kb, strip):
    t = pl.program_id(1)
    tb, width = u_ref.shape
    seg = tb // SUBLANE
    half = xs_ref.shape[1] // 2
    kw = width // n_kb
    sw = half // n_kb

    @pl.when(t == 0)
    def _():
        xc_ref[...] = x0_ref[0]

    idx0 = lax.broadcasted_iota(jnp.int32, (tb, tb), 0)
    idx1 = lax.broadcasted_iota(jnp.int32, (tb, tb), 1)
    to_segments = jnp.where(idx1 == (idx0 % SUBLANE) * seg + idx0 // SUBLANE, 1.0, 0.0).astype(BF16)
    to_time = jnp.where(idx0 == (idx1 % SUBLANE) * seg + idx1 // SUBLANE, 1.0, 0.0).astype(BF16)
    ub = jnp.dot(to_segments, u_ref[...].astype(BF16), preferred_element_type=F32).astype(BF16)
    for kb in range(n_kb):
        bu = jnp.dot(ub[:, kb * kw:(kb + 1) * kw], wb_ref[kb], preferred_element_type=F32)
        xs_ref[:, kb * sw:(kb + 1) * sw] = bu[:, :sw]
        xs_ref[:, half + kb * sw:half + (kb + 1) * sw] = bu[:, sw:]

    for c in range(half // strip):
        re = slice(c * strip, (c + 1) * strip)
        im = slice(half + c * strip, half + (c + 1) * strip)
        a_re = jnp.broadcast_to(pre_ref[0:1, re], (SUBLANE, strip))
        a_im = jnp.broadcast_to(pim_ref[0:1, re], (SUBLANE, strip))

        def local(j, carry):
            x_re, x_im = carry
            rows = pl.ds(pl.multiple_of(j * SUBLANE, SUBLANE), SUBLANE)
            n_re = a_re * x_re - a_im * x_im + xs_ref[rows, re]
            n_im = a_re * x_im + a_im * x_re + xs_ref[rows, im]
            xs_ref[rows, re] = n_re
            xs_ref[rows, im] = n_im
            return n_re, n_im

        zero = jnp.zeros((SUBLANE, strip), F32)
        e_re, e_im = lax.fori_loop(0, seg, local, (zero, zero), unroll=True)

        al_re, al_im = pre_ref[seg - 1:seg, re], pim_ref[seg - 1:seg, re]
        c_re, c_im = xc_ref[:, re], xc_ref[:, im]
        ins_re, ins_im = [], []
        for s in range(SUBLANE):
            ins_re.append(c_re)
            ins_im.append(c_im)
            c_re, c_im = (al_re * c_re - al_im * c_im + e_re[s:s + 1, :],
                          al_re * c_im + al_im * c_re + e_im[s:s + 1, :])
        xc_ref[:, re] = c_re
        xc_ref[:, im] = c_im
        in_re = jnp.concatenate(ins_re, axis=0)
        in_im = jnp.concatenate(ins_im, axis=0)

        def fix(j, carry):
            rows = pl.ds(pl.multiple_of(j * SUBLANE, SUBLANE), SUBLANE)
            w_re, w_im = pre_ref[pl.ds(j, 1), re], pim_ref[pl.ds(j, 1), re]
            xs_ref[rows, re] = xs_ref[rows, re] + (w_re * in_re - w_im * in_im)
            xs_ref[rows, im] = xs_ref[rows, im] + (w_re * in_im + w_im * in_re)
            return carry

        lax.fori_loop(0, seg, fix, 0, unroll=True)

    xl_ref[0] = xc_ref[...]
    for kb in range(n_kb):
        xb = jnp.concatenate([xs_ref[:, kb * sw:(kb + 1) * sw],
                              xs_ref[:, half + kb * sw:half + (kb + 1) * sw]], axis=1).astype(BF16)
        up_ref[:, kb * kw:(kb + 1) * kw] = jnp.dot(xb, wc_ref[kb], preferred_element_type=F32)
    y = _permute_rows(to_time, up_ref[...])
    z_ref[...] = jax.nn.gelu(y + d_ref[...] * u_ref[...])


def s5_scan(u_arr, row0, col0, n_seq, seq, x0, ld_re, ld_im, wb, wc, d_skip):
    n_kb, kw, two_sw = wb.shape
    width = n_kb * kw
    half = n_kb * two_sw // 2
    tb = _tile(seq, 128)
    nt = seq // tb
    seg = tb // SUBLANE
    assert row0 % tb == 0 and tb % SUBLANE == 0
    steps = jnp.arange(1, seg + 1, dtype=F32)[:, None]
    mag = jnp.exp(steps * ld_re)
    pow_re, pow_im = mag * jnp.cos(steps * ld_im), mag * jnp.sin(steps * ld_im)
    z, xl = pl.pallas_call(
        functools.partial(_s5_kernel, n_kb=n_kb, strip=min(1024, half)),
        grid=(n_seq, nt),
        in_specs=[pl.BlockSpec((tb, width), lambda s, t: (row0 // tb + s * nt + t, col0)),
                  pl.BlockSpec((1, 1, 2 * half), lambda s, t: (s, 0, 0)),
                  pl.BlockSpec((seg, half), lambda s, t: (0, 0)),
                  pl.BlockSpec((seg, half), lambda s, t: (0, 0)),
                  pl.BlockSpec(wb.shape, lambda s, t: (0, 0, 0)),
                  pl.BlockSpec(wc.shape, lambda s, t: (0, 0, 0)),
                  pl.BlockSpec((1, width), lambda s, t: (0, 0))],
        out_specs=[pl.BlockSpec((tb, width), lambda s, t: (s * nt + t, 0)),
                   pl.BlockSpec((1, 1, 2 * half), lambda s, t: (s, 0, 0))],
        out_shape=[jax.ShapeDtypeStruct((n_seq * seq, width), F32),
                   jax.ShapeDtypeStruct((n_seq, 1, 2 * half), F32)],
        scratch_shapes=[pltpu.VMEM((tb, width), F32), pltpu.VMEM((tb, 2 * half), F32),
                        pltpu.VMEM((1, 2 * half), F32)],
        compiler_params=_cparams("parallel", "arbitrary"),
        name="s5_scan",
    )(u_arr, x0, pow_re, pow_im, wb, wc, d_skip.reshape(1, -1))
    return z, xl.reshape(n_seq, 2 * half)


def s5_params(a_re, a_im, log_dt, b_re, b_im, c_re, c_im, kw):
    g, n, gc = b_re.shape
    gpb = kw // gc
    n_kb = g // gpb
    dt = jnp.exp(log_dt)[:, None]
    ld_re, ld_im = a_re * dt, a_im * dt
    e = jnp.exp(ld_re)
    abar_re, abar_im = e * jnp.cos(ld_im), e * jnp.sin(ld_im)
    den = a_re * a_re + a_im * a_im
    q_re = ((abar_re - 1.0) * a_re + abar_im * a_im) / den
    q_im = (abar_im * a_re - (abar_re - 1.0) * a_im) / den
    bb_re = q_re[..., None] * b_re - q_im[..., None] * b_im
    bb_im = q_re[..., None] * b_im + q_im[..., None] * b_re
    eye = jnp.eye(gpb, dtype=F32)

    def pack_b(bb):
        bb = bb.reshape(n_kb, gpb, n, gc)
        return jnp.einsum('kgnc,gh->kgchn', bb, eye).reshape(n_kb, gpb * gc, gpb * n)

    def pack_c(cc):
        cc = cc.reshape(n_kb, gpb, gc, n)
        return jnp.einsum('kgcn,gh->kgnhc', cc, eye).reshape(n_kb, gpb * n, gpb * gc)

    wb = jnp.concatenate([pack_b(bb_re), pack_b(bb_im)], axis=2).astype(BF16)
    wc = jnp.concatenate([pack_c(c_re), pack_c(-c_im)], axis=1).astype(BF16)
    return ld_re.reshape(1, g * n), ld_im.reshape(1, g * n), wb, wc


def _pad_cols(w, n):
    return jnp.pad(w, ((0, 0), (0, n - w.shape[1])))


def _layer(x, h1, cfg, cos, sin, st, p, g_after):
    seq, bsz, sq, past = cfg["seq"], cfg["bsz"], cfg["sq"], cfg["past"]
    t_all, d = x.shape
    n_s = bsz * sq
    width = cfg["mix_w"]
    n_heads, c_heads = cfg["a_heads"], cfg["c_heads"]

    layer = st["layer"]
    ff = p["ffn_w_down"].shape[2]

    def ffn(x, h, idx, g_post, g_next):
        hid = swiglu_up(h, p["ffn_w_up"], (layer, idx))
        y = matmul(hid, p["ffn_w_down"], b_sel=(layer, idx), out_dtype=F32, tm=512, tn=512,
                   tk=ff if ff <= 3 * 4096 else None, name="ffn_down")
        return add_rmsnorm(x, y, g_post, 0.5, g_next)

    g = p["norm_g"]
    x, h2 = ffn(x, h1, 0, g[1], g[2])

    a_cols = cfg["q_lora"] + cfg["kv_lora"] + cfg["a_rope"]
    cpad = _round_up(cfg["q_lora"] + cfg["kv_lora"] + LANE, width)
    c_cols = 3 * c_heads * LANE
    o_b, o_c, o_d, o_g = a_cols, a_cols + width, a_cols + width + c_cols, a_cols + 2 * width + c_cols
    w_in = p["w_in"]
    w_segf = jnp.concatenate([_pad_cols(w_in[:, :a_cols], cpad), w_in[:, o_b:o_c], w_in[:, o_d:o_g]],
                             axis=1).astype(BF16)
    segf = matmul(h2, w_segf, out_dtype=F32, name="w_in_f32")
    qkvc = matmul(h2, w_in[:, o_c:o_d].astype(BF16), out_dtype=F32, name="w_in_qkv")
    gates = matmul(h2, w_in[:, o_g:].astype(BF16), out_dtype=BF16, epilogue=_ep_sigmoid, name="w_in_gates")
    ub_col = cpad // width
    ud_col = ub_col + 1

    nope = LANE
    w_uq = p["mla_w_uq"].reshape(cfg["q_lora"], n_heads, nope + cfg["a_rope"])
    w_uq = jnp.pad(w_uq, ((0, 0), (0, 0), (0, 2 * LANE - nope - cfg["a_rope"])))
    w_uq = w_uq.reshape(cfg["q_lora"], n_heads * 2 * LANE).astype(BF16)
    w_ukv = p["mla_w_ukv"]
    cq, ckv, ckv_b, kr, kr_b = mla_prep(segf, p["mla_q_norm"], p["mla_kv_norm"], cos, sin, cpad)
    tm_q = _tile(t_all, 1024)
    q = matmul(cq, w_uq, out_dtype=BF16, tm=tm_q, tn=1024, name="mla_q",
               epilogue=functools.partial(_ep_rope_q, scale=(nope + cfg["a_rope"]) ** -0.5 * math.log2(math.e),
                                          half=cfg["a_rope"] // 2),
               extra=[(cos, (tm_q, LANE), lambda i, j: (i, 0)), (sin, (tm_q, LANE), lambda i, j: (i, 0))])
    tm_kv = _tile(seq, 1024)
    kv_p = matmul(ckv_b, w_ukv, b_sel=(layer,), out_dtype=BF16, m=seq, tm=tm_kv, widen=2, name="mla_kv_p",
                  epilogue=_ep_pack_kv,
                  extra=[(kr_b, (tm_kv, LANE), lambda i, j: (i, 0))])
    oa_p = mla_flash(q, kv_p, seq, n_heads)
    n_keys = past + sq
    kp = _round_up(n_keys, LANE)
    ckv_all = jnp.concatenate([st["ckv"].astype(BF16), ckv_b[seq:].reshape(bsz, sq, -1),
                               jnp.zeros((bsz, kp - n_keys, cfg["kv_lora"]), BF16)], axis=1)
    kr_all = jnp.concatenate([_pad_cols(st["kr"].reshape(bsz * past, -1), LANE).reshape(bsz, past, LANE).astype(BF16),
                              kr_b[seq:].reshape(bsz, sq, LANE), jnp.zeros((bsz, kp - n_keys, LANE), BF16)], axis=1)
    kv_s = matmul(ckv_all.reshape(bsz * kp, -1), w_ukv, b_sel=(layer,), out_dtype=BF16, name="mla_kv_s")
    oa_s = mla_sample(q, kv_s, kr_all.reshape(bsz * kp, LANE), seq, bsz, sq, n_heads, n_keys, past)
    o_a = jnp.concatenate([oa_p, oa_s], axis=0)

    lru = (p["lru_conv_w"], p["lru_conv_b"], p["lru_w_a"].astype(BF16), p["lru_b_a"],
           p["lru_w_x"].astype(BF16), p["lru_b_x"], p["lru_lambda"])
    n_conv = p["lru_conv_w"].shape[0] - 1
    ob_p, hl_p = rglru(segf, 0, ub_col, 1, seq, jnp.zeros((1, width), F32), jnp.zeros((1, n_conv, width), F32), *lru)
    ob_s, hl_s = rglru(segf, seq, ub_col, bsz, sq, st["h"], st["conv"], *lru)
    o_bb = jnp.concatenate([ob_p, ob_s], axis=0)
    conv_p = segf[seq - n_conv:seq, cpad:cpad + width][None]
    conv_s = segf[seq:, cpad:cpad + width].reshape(bsz, sq, width)[:, sq - n_conv:]

    win = BAND_CHUNKS * CHUNK
    tq_c = win
    assert seq % tq_c == 0
    bias_pp, bias_po = band_bias_tables(p["band_rel_bias"], tq_c, win, tq_c, tq_c)
    hps_p = 2 if c_heads % 2 == 0 else 1
    kcol, vcol = c_heads // hps_p, 2 * c_heads // hps_p
    prev_map = lambda col: (lambda i, g: (jnp.maximum(i - 1, 0), col + g))
    own_map = lambda col: (lambda i, g: (i, col + g))
    oc_p = band_attention(qkvc, 0, 0, qkvc, prev_map(kcol), qkvc, prev_map(vcol), qkvc, own_map(kcol),
                          qkvc, own_map(vcol), bias_pp, bias_po, n_heads=c_heads, hps=hps_p, n_blocks=seq // tq_c,
                          tq=tq_c, prev=win, own=tq_c, mask_first_prev=True)
    own_s = LANE
    k_new = qkvc[seq:, c_heads * LANE:2 * c_heads * LANE].reshape(bsz, sq, c_heads * LANE)
    v_new = qkvc[seq:, 2 * c_heads * LANE:].reshape(bsz, sq, c_heads * LANE)
    pad_own = lambda a: jnp.pad(a, ((0, 0), (0, own_s - sq), (0, 0))).reshape(bsz * own_s, c_heads * LANE)
    bias_sp, bias_so = band_bias_tables(p["band_rel_bias"], sq, win, own_s, sq)
    cache_k, cache_v, cache_blk0 = st["bk_all"], st["bv_all"], st["layer"] * bsz
    batch_map = lambda i, h: (i, h)
    cache_map = lambda i, h: (cache_blk0 + i, h)
    oc_s = band_attention(qkvc, seq, 0, cache_k, cache_map, cache_v, cache_map, pad_own(k_new), batch_map,
                          pad_own(v_new), batch_map, bias_sp, bias_so, n_heads=c_heads, hps=c_heads, n_blocks=bsz,
                          tq=sq, prev=win, own=own_s, mask_first_prev=False)
    o_cc = jnp.concatenate([oc_p, oc_s], axis=0)
    keep = min(win, seq)
    bk_p = qkvc[seq - keep:seq, c_heads * LANE:2 * c_heads * LANE].reshape(1, keep, c_heads, LANE)
    bv_p = qkvc[seq - keep:seq, 2 * c_heads * LANE:].reshape(1, keep, c_heads, LANE)
    bk_s = k_new.reshape(bsz, sq, c_heads, LANE)
    bv_s = v_new.reshape(bsz, sq, c_heads, LANE)

    n_grp, n_state = p["s5_a_re"].shape
    abar_re, abar_im, wb, wc = s5_params(p["s5_a_re"], p["s5_a_im"], p["s5_log_dt"], p["s5_b_re"], p["s5_b_im"],
                                         p["s5_c_re"], p["s5_c_im"], LANE)
    pack0 = lambda re, im: jnp.concatenate([re.reshape(-1, 1, n_grp * n_state), im.reshape(-1, 1, n_grp * n_state)],
                                           axis=2)
    zeros0 = jnp.zeros((1, n_grp, n_state), F32)
    z_p, xl_p = s5_scan(segf, 0, ud_col, 1, seq, pack0(zeros0, zeros0), abar_re, abar_im, wb, wc, p["s5_d"])
    z_s, xl_s = s5_scan(segf, seq, ud_col, bsz, sq, pack0(st["s5re"], st["s5im"]), abar_re, abar_im, wb, wc,
                        p["s5_d"])
    z = jnp.concatenate([z_p, z_s], axis=0)
    tm_g = _tile(t_all, 1024)
    tn_g = _tile(width, 1024)
    o_dd = matmul(z.astype(BF16), p["s5_w_glu"], b_sel=(layer,), out_dtype=BF16, tm=tm_g, tn=tn_g, epilogue=_ep_glu,
                  extra=[(z, (tm_g, tn_g), lambda i, j: (i, j)),
                         (p["s5_b_glu"].reshape(1, -1), (1, tn_g), lambda i, j: (0, j))], name="s5_glu")
    unpack = lambda xl, part: xl[:, part * n_grp * n_state:(part + 1) * n_grp * n_state].reshape(-1, n_grp, n_state)

    merged = merge_branches((o_a, o_bb, o_cc, o_dd), p["w_branch"], layer, gates)
    y = matmul(merged, p["w_out"], b_sel=(layer,), out_dtype=F32, name="w_out")
    x, h3 = add_rmsnorm(x, y, g[3], 1.0, g[4])
    if g_after is None:
        x, h_after = ffn(x, h3, 1, g[5], None), None
    else:
        x, h_after = ffn(x, h3, 1, g[5], g_after)

    new_p = (ckv[:seq][None], kr[:seq, :cfg["a_rope"]][None], hl_p, conv_p, bk_p, bv_p, unpack(xl_p, 0),
             unpack(xl_p, 1))
    new_s = (ckv[seq:].reshape(bsz, sq, -1), kr[seq:, :cfg["a_rope"]].reshape(bsz, sq, -1), hl_s, conv_s, bk_s, bv_s,
             unpack(xl_s, 0), unpack(xl_s, 1))
    return x, h_after, new_p, new_s


def kernel(x_prompt, x_sample, cache_mla_ckv, cache_mla_krope, state_lru_h, state_lru_conv, cache_band_k, cache_band_v, state_s5_re, state_s5_im, norm_g, ffn_w_up, ffn_w_down, w_in, mla_q_norm, mla_kv_norm, mla_w_uq, mla_w_ukv, lru_conv_w, lru_conv_b, lru_w_a, lru_b_a, lru_w_x, lru_b_x, lru_lambda, band_rel_bias, s5_a_re, s5_a_im, s5_log_dt, s5_b_re, s5_b_im, s5_c_re, s5_c_im, s5_d, s5_w_glu, s5_b_glu, w_branch, w_out):
    depth = norm_g.shape[0]
    n_p, seq, d = x_prompt.shape
    bsz, sq, _ = x_sample.shape
    past = cache_mla_ckv.shape[2]
    a_rope = cache_mla_krope.shape[3]
    mix_w = w_branch.shape[2]
    a_heads = mix_w // LANE
    c_heads = cache_band_k.shape[3]
    assert n_p == 1 and a_rope == 2 * ROPE_HALF and cache_band_k.shape[4] == LANE
    assert mla_w_uq.shape[2] == a_heads * (LANE + a_rope) and mla_w_ukv.shape[2] == a_heads * 2 * LANE
    assert past % CHUNK == 0 and cache_band_k.shape[2] == BAND_CHUNKS * CHUNK and past >= BAND_CHUNKS * CHUNK
    assert sq <= CHUNK and seq % (BAND_CHUNKS * CHUNK) == 0 and sq >= lru_conv_w.shape[1] - 1
    cfg = dict(seq=seq, bsz=bsz, sq=sq, past=past, mix_w=mix_w, a_heads=a_heads, c_heads=c_heads,
               q_lora=mla_q_norm.shape[1], kv_lora=mla_kv_norm.shape[1], a_rope=a_rope)

    pos = np.concatenate([np.arange(seq), np.tile(past + np.arange(sq), bsz)]).astype(np.float32)
    half = a_rope // 2
    inv = ROPE_THETA ** (-jnp.arange(half, dtype=F32) / half)
    ang = jnp.asarray(pos)[:, None] * inv
    zpad = jnp.zeros((pos.shape[0], LANE - a_rope), F32)
    cos = jnp.concatenate([jnp.cos(ang), jnp.cos(ang), zpad], axis=1)
    sin = jnp.concatenate([-jnp.sin(ang), jnp.sin(ang), zpad], axis=1)

    x = jnp.concatenate([x_prompt.reshape(seq, d), x_sample.reshape(bsz * sq, d)], axis=0)
    h = rmsnorm(x, norm_g[0, 0])
    stacked = dict(ffn_w_up=ffn_w_up.astype(BF16), ffn_w_down=ffn_w_down.astype(BF16),
                   mla_w_ukv=mla_w_ukv.astype(BF16), s5_w_glu=s5_w_glu.astype(BF16),
                   w_branch=w_branch.astype(BF16), w_out=w_out.astype(BF16))
    new_p, new_s = [], []
    for l in range(depth):
        st = dict(ckv=cache_mla_ckv[l], kr=cache_mla_krope[l], h=state_lru_h[l], conv=state_lru_conv[l],
                  bk_all=cache_band_k.reshape(-1, c_heads * LANE), bv_all=cache_band_v.reshape(-1, c_heads * LANE),
                  layer=l, s5re=state_s5_re[l], s5im=state_s5_im[l])
        p = dict(stacked, norm_g=norm_g[l], w_in=w_in[l],
                 mla_q_norm=mla_q_norm[l], mla_kv_norm=mla_kv_norm[l], mla_w_uq=mla_w_uq[l],
                 lru_conv_w=lru_conv_w[l], lru_conv_b=lru_conv_b[l], lru_w_a=lru_w_a[l], lru_b_a=lru_b_a[l],
                 lru_w_x=lru_w_x[l], lru_b_x=lru_b_x[l], lru_lambda=lru_lambda[l], band_rel_bias=band_rel_bias[l],
                 s5_a_re=s5_a_re[l], s5_a_im=s5_a_im[l], s5_log_dt=s5_log_dt[l], s5_b_re=s5_b_re[l],
                 s5_b_im=s5_b_im[l], s5_c_re=s5_c_re[l], s5_c_im=s5_c_im[l], s5_d=s5_d[l], s5_b_glu=s5_b_glu[l])
        x, h, st_p, st_s = _layer(x, h, cfg, cos, sin, st, p, norm_g[l + 1, 0] if l + 1 < depth else None)
        new_p.append(st_p)
        new_s.append(st_s)

    outs = [x[:seq].reshape(1, seq, d), x[seq:].reshape(bsz, sq, d)]
    for i in range(8):
        outs.append(jnp.stack([s[i] for s in new_p], axis=0))
        outs.append(jnp.stack([s[i] for s in new_s], axis=0))
    return tuple(outs)
```

```python
import functools
import math

import jax
import jax.numpy as jnp
import numpy as np
from jax import lax
from jax.experimental import pallas as pl
from jax.experimental.pallas import tpu as pltpu

F32 = jnp.float32
BF16 = jnp.bfloat16

CHUNK = 64
BAND_CHUNKS = 8
EPS = 1e-6
ROPE_THETA = 10000.0
LRU_C = 8.0
ROPE_HALF = 32
MASK_VALUE = -1e30
LANE = 128
SUBLANE = 8
VMEM_LIMIT = 56 * 1024 * 1024


def _cparams(*sem):
    return pltpu.CompilerParams(dimension_semantics=sem, vmem_limit_bytes=VMEM_LIMIT)


def _round_up(x, m):
    return (x + m - 1) // m * m


def _tile(dim, pref):
    if dim <= pref:
        return dim
    t = pref
    while dim % t:
        t //= 2
    assert t >= SUBLANE, (dim, pref)
    return t


def _rmsnorm_kernel(x_ref, g_ref, o_ref):
    x = x_ref[...]
    ms = jnp.mean(x * x, axis=-1, keepdims=True)
    o_ref[...] = (x * lax.rsqrt(ms + EPS) * g_ref[...]).astype(o_ref.dtype)


def rmsnorm(x, g, out_dtype=BF16):
    t, d = x.shape
    tm = _tile(t, 256)
    return pl.pallas_call(
        _rmsnorm_kernel,
        grid=(t // tm,),
        in_specs=[pl.BlockSpec((tm, d), lambda i: (i, 0)), pl.BlockSpec((1, d), lambda i: (0, 0))],
        out_specs=pl.BlockSpec((tm, d), lambda i: (i, 0)),
        out_shape=jax.ShapeDtypeStruct((t, d), out_dtype),
        compiler_params=_cparams("parallel"),
        name="rmsnorm",
    )(x, g.reshape(1, d))


def _add_rmsnorm_kernel(x_ref, y_ref, g_ref, *rest, scale, with_next):
    y = y_ref[...]
    ms = jnp.mean(y * y, axis=-1, keepdims=True)
    x = x_ref[...] + scale * (y * lax.rsqrt(ms + EPS) * g_ref[...])
    if not with_next:
        rest[0][...] = x
        return
    gn_ref, o_ref, h_ref = rest
    o_ref[...] = x
    ms = jnp.mean(x * x, axis=-1, keepdims=True)
    h_ref[...] = (x * lax.rsqrt(ms + EPS) * gn_ref[...]).astype(h_ref.dtype)


def add_rmsnorm(x, y, g, scale, g_next=None):
    t, d = x.shape
    tm = _tile(t, 256)
    row = pl.BlockSpec((tm, d), lambda i: (i, 0))
    vec = pl.BlockSpec((1, d), lambda i: (0, 0))
    with_next = g_next is not None
    return pl.pallas_call(
        functools.partial(_add_rmsnorm_kernel, scale=scale, with_next=with_next),
        grid=(t // tm,),
        in_specs=[row, row, vec] + ([vec] if with_next else []),
        out_specs=[row, row] if with_next else row,
        out_shape=([jax.ShapeDtypeStruct((t, d), F32), jax.ShapeDtypeStruct((t, d), BF16)] if with_next
                   else jax.ShapeDtypeStruct((t, d), F32)),
        compiler_params=_cparams("parallel"),
        name="add_rmsnorm",
    )(x, y, g.reshape(1, d), *([g_next.reshape(1, d)] if with_next else []))


def _k_tile(k):
    if k <= 4096:
        return k
    for parts in range(2, k // LANE + 1):
        if k % parts == 0 and (k // parts) % LANE == 0 and k // parts <= 4096:
            return k // parts
    raise ValueError(k)


def _mm_kernel(*refs, nk, n_extra, epilogue):
    a_ref, b_ref = refs[0], refs[1]
    extra = refs[2:2 + n_extra]
    o_ref = refs[2 + n_extra]
    if nk == 1:
        acc = jnp.dot(a_ref[...], b_ref[...], preferred_element_type=F32)
        o_ref[...] = epilogue(acc, *extra).astype(o_ref.dtype)
        return
    acc_ref = refs[3 + n_extra]
    k = pl.program_id(2)

    @pl.when(k == 0)
    def _():
        acc_ref[...] = jnp.zeros_like(acc_ref)

    acc_ref[...] += jnp.dot(a_ref[...], b_ref[...], preferred_element_type=F32)

    @pl.when(k == nk - 1)
    def _():
        o_ref[...] = epilogue(acc_ref[...], *extra).astype(o_ref.dtype)


def _ep_none(acc):
    return acc


def _ep_sigmoid(acc):
    return jax.nn.sigmoid(acc)


def _ep_glu(acc, z_ref, b_ref):
    z = z_ref[...]
    return z * jax.nn.sigmoid(acc + b_ref[...])


def _ep_pack_kv(acc, kr_ref):
    kr = kr_ref[...].astype(F32)
    ones = jnp.ones_like(kr)
    outs = []
    for h in range(acc.shape[1] // (2 * LANE)):
        outs += [acc[:, h * 2 * LANE:h * 2 * LANE + LANE], kr, acc[:, h * 2 * LANE + LANE:(h + 1) * 2 * LANE], ones]
    return jnp.concatenate(outs, axis=1)


def _ep_rope_q(acc, cos_ref, sin_ref, *, scale, half):
    cos, sin = cos_ref[...], sin_ref[...]
    lane = lax.broadcasted_iota(jnp.int32, cos.shape, 1)
    outs = []
    for h in range(acc.shape[1] // (2 * LANE)):
        nope = acc[:, h * 2 * LANE:h * 2 * LANE + LANE]
        r = acc[:, h * 2 * LANE + LANE:(h + 1) * 2 * LANE]
        swapped = jnp.where(lane < half, pltpu.roll(r, LANE - half, axis=1), pltpu.roll(r, half, axis=1))
        outs += [nope * scale, (r * cos + swapped * sin) * scale]
    return jnp.concatenate(outs, axis=1)


def matmul(a, b, *, out_dtype, m=None, tm=1024, tn=1024, tk=None, b_sel=(), epilogue=_ep_none, extra=(), widen=1,
           name="matmul"):
    m = a.shape[0] if m is None else m
    k, n = b.shape[-2:]
    assert a.shape[1] == k and b.ndim == 2 + len(b_sel)
    tm, tn, tk = _tile(m, tm), _tile(n, tn), (_k_tile(k) if tk is None else tk)
    nk = k // tk
    lead = (None,) * len(b_sel)
    if nk == 1:
        grid = (m // tm, n // tn)
        wrap = lambda f: f
        a_spec = pl.BlockSpec((tm, k), lambda i, j: (i, 0))
        b_spec = pl.BlockSpec(lead + (k, tn), lambda i, j: (*b_sel, 0, j))
        o_spec = pl.BlockSpec((tm, tn * widen), lambda i, j: (i, j))
        scratch = []
        sem = ("parallel", "parallel")
    else:
        grid = (m // tm, n // tn, nk)
        wrap = lambda f: (lambda i, j, kk: f(i, j))
        a_spec = pl.BlockSpec((tm, tk), lambda i, j, kk: (i, kk))
        b_spec = pl.BlockSpec(lead + (tk, tn), lambda i, j, kk: (*b_sel, kk, j))
        o_spec = pl.BlockSpec((tm, tn * widen), lambda i, j, kk: (i, j))
        scratch = [pltpu.VMEM((tm, tn), F32)]
        sem = ("parallel", "parallel", "arbitrary")
    extra_specs = [pl.BlockSpec(bs, wrap(im)) for _, bs, im in extra]
    return pl.pallas_call(
        functools.partial(_mm_kernel, nk=nk, n_extra=len(extra), epilogue=epilogue),
        grid=grid,
        in_specs=[a_spec, b_spec] + extra_specs,
        out_specs=o_spec,
        out_shape=jax.ShapeDtypeStruct((m, n * widen), out_dtype),
        scratch_shapes=scratch,
        compiler_params=_cparams(*sem),
        name=name,
    )(a, b, *[e[0] for e in extra])


def _swiglu_kernel(a_ref, wg_ref, wu_ref, o_ref):
    a = a_ref[...]
    gate = jnp.dot(a, wg_ref[...], preferred_element_type=F32)
    up = jnp.dot(a, wu_ref[...], preferred_element_type=F32)
    o_ref[...] = (jax.nn.silu(gate) * up).astype(o_ref.dtype)


def swiglu_up(a, w_up, sel):
    t, d = a.shape
    ff = w_up.shape[-1] // 2
    cap = (36 * 1024 * 1024) // (4 * d)
    tall = [r for r in range(LANE, cap + 1, LANE) if t % r == 0]
    tm, tn = (tall[-1] if tall else _tile(t, 1024)), _tile(ff, 512)
    nj = ff // tn
    lead = (None,) * len(sel)
    return pl.pallas_call(
        _swiglu_kernel,
        grid=(t // tm, nj),
        in_specs=[pl.BlockSpec((tm, d), lambda i, j: (i, 0)),
                  pl.BlockSpec(lead + (d, tn), lambda i, j: (*sel, 0, j)),
                  pl.BlockSpec(lead + (d, tn), lambda i, j: (*sel, 0, j + nj))],
        out_specs=pl.BlockSpec((tm, tn), lambda i, j: (i, j)),
        out_shape=jax.ShapeDtypeStruct((t, ff), BF16),
        compiler_params=_cparams("parallel", "parallel"),
        name="swiglu_up",
    )(a, w_up, w_up)


def _merge_kernel(oa_ref, ob_ref, oc_ref, od_ref, wb_ref, ga_ref, gb_ref, gc_ref, gd_ref, o_ref):
    total = None
    for i, (o_r, g_r) in enumerate(((oa_ref, ga_ref), (ob_ref, gb_ref), (oc_ref, gc_ref), (od_ref, gd_ref))):
        part = g_r[...].astype(F32) * jnp.dot(o_r[...], wb_ref[i], preferred_element_type=F32)
        total = part if total is None else total + part
    o_ref[...] = total.astype(o_ref.dtype)


def merge_branches(outs, w_branch, layer, gates):
    t, w = outs[0].shape
    d = w_branch.shape[3]
    tm, tn = _tile(t, 512), _tile(d, 512)
    nj = d // tn
    o_spec = pl.BlockSpec((tm, w), lambda i, j: (i, 0))
    g_specs = [pl.BlockSpec((tm, tn), functools.partial(lambda i, j, b: (i, j + b * nj), b=b)) for b in range(4)]
    return pl.pallas_call(
        _merge_kernel,
        grid=(t // tm, nj),
        in_specs=[o_spec] * 4 + [pl.BlockSpec((None, 4, w, tn), lambda i, j: (layer, 0, 0, j))] + g_specs,
        out_specs=pl.BlockSpec((tm, tn), lambda i, j: (i, j)),
        out_shape=jax.ShapeDtypeStruct((t, d), BF16),
        compiler_params=_cparams("parallel", "parallel"),
        name="merge_branches",
    )(*outs, w_branch, gates, gates, gates, gates)


def _mla_prep_kernel(c_ref, qn_ref, kvn_ref, cos_ref, sin_ref, cq_ref, ckv_ref, ckvb_ref, kr_ref, krb_ref,
                     *, q_lora, kv_lora, half):
    c_q = c_ref[:, :q_lora]
    ms = jnp.mean(c_q * c_q, axis=-1, keepdims=True)
    cq_ref[...] = (c_q * lax.rsqrt(ms + EPS) * qn_ref[...]).astype(cq_ref.dtype)
    c_kv = c_ref[:, q_lora:q_lora + kv_lora]
    ms = jnp.mean(c_kv * c_kv, axis=-1, keepdims=True)
    ckv = c_kv * lax.rsqrt(ms + EPS) * kvn_ref[...]
    ckv_ref[...] = ckv
    ckvb_ref[...] = ckv.astype(ckvb_ref.dtype)
    r = c_ref[:, q_lora + kv_lora:q_lora + kv_lora + LANE]
    lane = lax.broadcasted_iota(jnp.int32, r.shape, 1)
    swapped = jnp.where(lane < half, pltpu.roll(r, LANE - half, axis=1), pltpu.roll(r, half, axis=1))
    kr = r * cos_ref[...] + swapped * sin_ref[...]
    kr_ref[...] = kr
    krb_ref[...] = kr.astype(krb_ref.dtype)


def mla_prep(segf, q_norm, kv_norm, cos, sin, cpad):
    t = segf.shape[0]
    q_lora, kv_lora = q_norm.shape[0], kv_norm.shape[0]
    tm = _tile(t, 256)
    row = lambda w: pl.BlockSpec((tm, w), lambda i: (i, 0))
    const = lambda w: pl.BlockSpec((1, w), lambda i: (0, 0))
    return pl.pallas_call(
        functools.partial(_mla_prep_kernel, q_lora=q_lora, kv_lora=kv_lora, half=ROPE_HALF),
        grid=(t // tm,),
        in_specs=[row(cpad), const(q_lora), const(kv_lora), row(LANE), row(LANE)],
        out_specs=[row(q_lora), row(kv_lora), row(kv_lora), row(LANE), row(LANE)],
        out_shape=[jax.ShapeDtypeStruct((t, q_lora), BF16), jax.ShapeDtypeStruct((t, kv_lora), F32),
                   jax.ShapeDtypeStruct((t, kv_lora), BF16), jax.ShapeDtypeStruct((t, LANE), F32),
                   jax.ShapeDtypeStruct((t, LANE), BF16)],
        compiler_params=_cparams("parallel"),
        name="mla_prep",
    )(segf, q_norm.reshape(1, -1), kv_norm.reshape(1, -1), cos, sin)


def _mla_flash_kernel(qi_ref, ki_ref, q_ref, kv_ref, o_ref, m_ref, acc_ref, *, heads, tq, tk):
    s_id = pl.program_id(1)
    qi, ki = qi_ref[s_id], ki_ref[s_id]
    hw = 2 * LANE

    @pl.when(ki == 0)
    def _():
        m_ref[...] = jnp.full_like(m_ref, -jnp.inf)
        acc_ref[...] = jnp.zeros_like(acc_ref)

    def step(masked):
        if masked:
            rc = lax.broadcasted_iota(jnp.int32, (tq, tk), 0) // CHUNK
            cc = lax.broadcasted_iota(jnp.int32, (tq, tk), 1) // CHUNK
            visible = cc <= rc
        for h in range(heads):
            q = q_ref[:, h * hw:(h + 1) * hw]
            k = kv_ref[:, 2 * h * hw:(2 * h + 1) * hw]
            v1 = kv_ref[:, (2 * h + 1) * hw:(2 * h + 2) * hw]
            s = lax.dot_general(q, k, (((1,), (1,)), ((), ())), preferred_element_type=F32)
            if masked:
                s = jnp.where(visible, s, MASK_VALUE)
            m_prev = m_ref[h]
            m_next = jnp.maximum(m_prev, jnp.max(s, axis=1)[:, None])
            p = jnp.exp2(s - jnp.tile(m_next, (1, tk // LANE)))
            alpha = jnp.exp2(m_prev - m_next)
            m_ref[h] = m_next
            pv = jnp.dot(p.astype(BF16), v1, preferred_element_type=F32)
            acc_ref[:, h * hw:(h + 1) * hw] = jnp.tile(alpha, (1, 2)) * acc_ref[:, h * hw:(h + 1) * hw] + pv

    @pl.when(ki < qi)
    def _():
        step(False)

    @pl.when(ki == qi)
    def _():
        step(True)
        for h in range(heads):
            o_ref[:, h * LANE:(h + 1) * LANE] = (
                acc_ref[:, h * hw:h * hw + LANE] / acc_ref[:, h * hw + LANE:(h + 1) * hw]).astype(o_ref.dtype)


def mla_flash(q, kv, seq, n_heads, *, tile=1024, heads_per_step=4):
    tq = tk = _tile(seq, tile)
    hg = min(heads_per_step, n_heads)
    assert n_heads % hg == 0 and tq % CHUNK == 0
    nq = seq // tq
    pairs = [(i, j) for i in range(nq) for j in range(i + 1)]
    qi_arr = jnp.asarray(np.array([p[0] for p in pairs], np.int32))
    ki_arr = jnp.asarray(np.array([p[1] for p in pairs], np.int32))
    grid_spec = pltpu.PrefetchScalarGridSpec(
        num_scalar_prefetch=2,
        grid=(n_heads // hg, len(pairs)),
        in_specs=[pl.BlockSpec((tq, hg * 2 * LANE), lambda g, s, qi, ki: (qi[s], g)),
                  pl.BlockSpec((tk, hg * 4 * LANE), lambda g, s, qi, ki: (ki[s], g))],
        out_specs=pl.BlockSpec((tq, hg * LANE), lambda g, s, qi, ki: (qi[s], g)),
        scratch_shapes=[pltpu.VMEM((hg, tq, LANE), F32), pltpu.VMEM((tq, hg * 2 * LANE), F32)],
    )
    return pl.pallas_call(
        functools.partial(_mla_flash_kernel, heads=hg, tq=tq, tk=tk),
        grid_spec=grid_spec,
        out_shape=jax.ShapeDtypeStruct((seq, n_heads * LANE), BF16),
        compiler_params=_cparams("parallel", "arbitrary"),
        name="mla_flash",
    )(qi_arr, ki_arr, q, kv)


def _mla_sample_kernel(q_ref, kv_ref, kr_ref, o_ref, *, heads, n_keys, past):
    sq, kp = q_ref.shape[0], kv_ref.shape[0]
    kr = kr_ref[...]
    qpos = past + lax.broadcasted_iota(jnp.int32, (sq, kp), 0)
    kpos = lax.broadcasted_iota(jnp.int32, (sq, kp), 1)
    visible = (kpos < n_keys) & (kpos // CHUNK <= qpos // CHUNK)
    for h in range(heads):
        q = q_ref[:, h * 2 * LANE:(h + 1) * 2 * LANE]
        k = jnp.concatenate([kv_ref[:, h * 2 * LANE:h * 2 * LANE + LANE], kr], axis=1)
        v = kv_ref[:, h * 2 * LANE + LANE:(h + 1) * 2 * LANE]
        s = lax.dot_general(q, k, (((1,), (1,)), ((), ())), preferred_element_type=F32)
        s = jnp.where(visible, s, MASK_VALUE)
        p = jnp.exp2(s - jnp.max(s, axis=-1, keepdims=True))
        denom = jnp.sum(p, axis=-1, keepdims=True)
        pv = jnp.dot(p.astype(v.dtype), v, preferred_element_type=F32)
        o_ref[:, h * LANE:(h + 1) * LANE] = (pv / denom).astype(o_ref.dtype)


def mla_sample(q, kv, kr, row0, bsz, sq, n_heads, n_keys, past):
    kp = kv.shape[0] // bsz
    assert row0 % sq == 0
    return pl.pallas_call(
        functools.partial(_mla_sample_kernel, heads=n_heads, n_keys=n_keys, past=past),
        grid=(bsz,),
        in_specs=[pl.BlockSpec((sq, n_heads * 2 * LANE), lambda b: (row0 // sq + b, 0)),
                  pl.BlockSpec((kp, n_heads * 2 * LANE), lambda b: (b, 0)),
                  pl.BlockSpec((kp, LANE), lambda b: (b, 0))],
        out_specs=pl.BlockSpec((sq, n_heads * LANE), lambda b: (b, 0)),
        out_shape=jax.ShapeDtypeStruct((bsz * sq, n_heads * LANE), BF16),
        compiler_params=_cparams("parallel"),
        name="mla_sample",
    )(q, kv, kr)


def _band_kernel(q_ref, kp_ref, ko_ref, vp_ref, vo_ref, bp_ref, bo_ref, o_ref, *, heads, scale, mask_first_prev):
    dn = (((1,), (1,)), ((), ()))
    ones_p = jnp.ones((kp_ref.shape[0], LANE), BF16)
    ones_o = jnp.ones((ko_ref.shape[0], LANE), BF16)
    for h in range(heads):
        cs = slice(h * LANE, (h + 1) * LANE)
        q = (q_ref[:, cs] * scale).astype(BF16)
        sp = lax.dot_general(q, kp_ref[:, cs].astype(BF16), dn, preferred_element_type=F32) + bp_ref[h]
        so = lax.dot_general(q, ko_ref[:, cs].astype(BF16), dn, preferred_element_type=F32) + bo_ref[h]
        if mask_first_prev:
            sp = jnp.where(pl.program_id(1) > 0, sp, MASK_VALUE)
        m = jnp.maximum(jnp.max(sp, axis=-1, keepdims=True), jnp.max(so, axis=-1, keepdims=True))
        pp, po = jnp.exp2(sp - m), jnp.exp2(so - m)
        vp1 = jnp.concatenate([vp_ref[:, cs].astype(BF16), ones_p], axis=1)
        vo1 = jnp.concatenate([vo_ref[:, cs].astype(BF16), ones_o], axis=1)
        pv = (jnp.dot(pp.astype(BF16), vp1, preferred_element_type=F32)
              + jnp.dot(po.astype(BF16), vo1, preferred_element_type=F32))
        o_ref[:, cs] = (pv[:, :LANE] / pv[:, LANE:]).astype(o_ref.dtype)


def band_attention(q_arr, q_row0, q_col0, kp_arr, kp_map, vp_arr, vp_map, ko_arr, ko_map, vo_arr, vo_map,
                   bias_p, bias_o, *, n_heads, hps, n_blocks, tq, prev, own, mask_first_prev):
    hd = LANE
    w = hps * hd
    assert q_row0 % tq == 0 and n_heads % hps == 0
    return pl.pallas_call(
        functools.partial(_band_kernel, heads=hps, scale=hd ** -0.5 * math.log2(math.e),
                          mask_first_prev=mask_first_prev),
        grid=(n_heads // hps, n_blocks),
        in_specs=[pl.BlockSpec((tq, w), lambda g, i: (q_row0 // tq + i, q_col0 + g)),
                  pl.BlockSpec((prev, w), lambda g, i: kp_map(i, g)),
                  pl.BlockSpec((own, w), lambda g, i: ko_map(i, g)),
                  pl.BlockSpec((prev, w), lambda g, i: vp_map(i, g)),
                  pl.BlockSpec((own, w), lambda g, i: vo_map(i, g)),
                  pl.BlockSpec((hps, tq, prev), lambda g, i: (g, 0, 0)),
                  pl.BlockSpec((hps, tq, own), lambda g, i: (g, 0, 0))],
        out_specs=pl.BlockSpec((tq, w), lambda g, i: (i, g)),
        out_shape=jax.ShapeDtypeStruct((n_blocks * tq, n_heads * hd), BF16),
        compiler_params=_cparams("parallel", "arbitrary"),
        name="band_attention",
    )(q_arr, kp_arr, ko_arr, vp_arr, vo_arr, bias_p, bias_o)


def band_bias_tables(rel_bias, tq, prev, own, own_valid):
    n_heads = rel_bias.shape[0]
    clip = (rel_bias.shape[1] - 1) // 2
    qr = np.arange(tq)[:, None]

    def table(krel, valid):
        nk, k0 = krel.shape[1], int(krel[0, 0])
        period = nk + tq
        e = np.zeros(period, np.int64)
        e[:nk] = -np.arange(nk)
        e[nk + 1:] = np.arange(tq - 1, 0, -1)
        per_offset = jnp.take(rel_bias, jnp.asarray(np.clip(e - k0, -clip, clip) + clip), axis=1)
        toeplitz = jnp.tile(per_offset, (1, tq))[:, :tq * (period - 1)].reshape(n_heads, tq, period - 1)[:, :, :nk]
        dc = qr // CHUNK - np.floor_divide(krel, CHUNK)
        vis = (dc >= 0) & (dc <= BAND_CHUNKS) & valid
        return jnp.where(jnp.asarray(vis)[None], toeplitz * math.log2(math.e), MASK_VALUE)

    kp = np.arange(-prev, 0)[None, :]
    ko = np.arange(own)[None, :]
    return table(kp, np.ones_like(kp, bool)), table(ko, ko < own_valid)


def _rglru_kernel(u_ref, h0_ref, c0_ref, cw_ref, cb_ref, wa_ref, ba_ref, wx_ref, bx_ref, lam_ref,
                  o_ref, hl_ref, prev_ref, h_ref, a_ref, b_ref, *, n_blocks, bw):
    t = pl.program_id(1)
    tb, width = u_ref.shape

    @pl.when(t == 0)
    def _():
        prev_ref[...] = c0_ref[0]
        h_ref[...] = h0_ref[0]

    u = u_ref[...]
    cw = cw_ref[...]
    nw = 4
    xc = cb_ref[...] + cw[nw - 1:nw, :] * u
    for s in range(1, nw):
        xc = xc + cw[nw - 1 - s:nw - s, :] * pltpu.roll(u, s, axis=0)
    head = jnp.concatenate([prev_ref[...], u[:SUBLANE, :]], axis=0)
    xh = cb_ref[...] + cw[nw - 1:nw, :] * u[:SUBLANE, :]
    for s in range(1, nw):
        xh = xh + cw[nw - 1 - s:nw - s, :] * pltpu.roll(head, s, axis=0)[SUBLANE:, :]
    a_ref[:SUBLANE, :] = xh
    a_ref[SUBLANE:, :] = xc[SUBLANE:, :]
    xc = a_ref[...]
    prev_ref[...] = u[tb - SUBLANE:, :]

    neg_c_sp = -LRU_C * jax.nn.softplus(-lam_ref[...])
    for n in range(n_blocks):
        sl = slice(n * bw, (n + 1) * bw)
        xb = xc[:, sl].astype(BF16)
        r = jax.nn.sigmoid(jnp.dot(xb, wa_ref[n], preferred_element_type=F32) + ba_ref[:, sl])
        i = jax.nn.sigmoid(jnp.dot(xb, wx_ref[n], preferred_element_type=F32) + bx_ref[:, sl])
        log_a = neg_c_sp[:, sl] * r
        a = jnp.exp(log_a)
        a_ref[:, sl] = a
        b_ref[:, sl] = jnp.sqrt(-jnp.tanh(log_a) * (a * a + 1.0)) * (i * xc[:, sl])

    def row(j, h):
        h = a_ref[pl.ds(j, 1), :] * h + b_ref[pl.ds(j, 1), :]
        b_ref[pl.ds(j, 1), :] = h
        return h

    h = lax.fori_loop(0, tb, row, h_ref[...], unroll=8)
    h_ref[...] = h
    o_ref[...] = b_ref[...].astype(o_ref.dtype)
    hl_ref[0] = h


def rglru(u_arr, row0, col0, n_seq, seq, h0, conv0, conv_w, conv_b, w_a, b_a, w_x, b_x, lam):
    width = h0.shape[-1]
    n_blocks, bw = w_a.shape[0], w_a.shape[1]
    tb = _tile(seq, 256)
    nt = seq // tb
    assert row0 % tb == 0 and tb >= 2 * SUBLANE and conv_w.shape[0] == 4
    conv0p = jnp.concatenate([jnp.zeros((n_seq, SUBLANE - conv0.shape[1], width), F32), conv0], axis=1)
    cwp = jnp.concatenate([conv_w, jnp.zeros((SUBLANE - conv_w.shape[0], width), F32)], axis=0)
    vec = lambda: pl.BlockSpec((1, width), lambda s, t: (0, 0))
    wspec = lambda: pl.BlockSpec((n_blocks, bw, bw), lambda s, t: (0, 0, 0))
    out, hl = pl.pallas_call(
        functools.partial(_rglru_kernel, n_blocks=n_blocks, bw=bw),
        grid=(n_seq, nt),
        in_specs=[pl.BlockSpec((tb, width), lambda s, t: (row0 // tb + s * nt + t, col0)),
                  pl.BlockSpec((1, 1, width), lambda s, t: (s, 0, 0)),
                  pl.BlockSpec((1, SUBLANE, width), lambda s, t: (s, 0, 0)),
                  pl.BlockSpec((SUBLANE, width), lambda s, t: (0, 0)),
                  vec(), wspec(), vec(), wspec(), vec(), vec()],
        out_specs=[pl.BlockSpec((tb, width), lambda s, t: (s * nt + t, 0)),
                   pl.BlockSpec((1, 1, width), lambda s, t: (s, 0, 0))],
        out_shape=[jax.ShapeDtypeStruct((n_seq * seq, width), BF16),
                   jax.ShapeDtypeStruct((n_seq, 1, width), F32)],
        scratch_shapes=[pltpu.VMEM((SUBLANE, width), F32), pltpu.VMEM((1, width), F32),
                        pltpu.VMEM((tb, width), F32), pltpu.VMEM((tb, width), F32)],
        compiler_params=_cparams("parallel", "arbitrary"),
        name="rglru",
    )(u_arr, h0.reshape(n_seq, 1, width), conv0p, cwp, conv_b.reshape(1, -1), w_a, b_a.reshape(1, -1),
      w_x, b_x.reshape(1, -1), lam.reshape(1, -1))
    return out, hl.reshape(n_seq, width)


def _permute_rows(perm, x):
    hi = x.astype(BF16)
    rest = x - hi.astype(F32)
    mid = rest.astype(BF16)
    lo = (rest - mid.astype(F32)).astype(BF16)
    dot = lambda piece: jnp.dot(perm, piece, preferred_element_type=F32)
    return (dot(hi) + dot(mid)) + dot(lo)


def _s5_kernel(u_ref, x0_ref, pre_ref, pim_ref, wb_ref, wc_ref, d_ref, z_ref, xl_ref, up_ref, xs_ref, xc_ref,
               *, n_kb, strip):
    t = pl.program_id(1)
    tb, width = u_ref.shape
    seg = tb // SUBLANE
    half = xs_ref.shape[1] // 2
    kw = width // n_kb
    sw = half // n_kb

    @pl.when(t == 0)
    def _():
        xc_ref[...] = x0_ref[0]

    idx0 = lax.broadcasted_iota(jnp.int32, (tb, tb), 0)
    idx1 = lax.broadcasted_iota(jnp.int32, (tb, tb), 1)
    to_segments = jnp.where(idx1 == (idx0 % SUBLANE) * seg + idx0 // SUBLANE, 1.0, 0.0).astype(BF16)
    to_time = jnp.where(idx0 == (idx1 % SUBLANE) * seg + idx1 // SUBLANE, 1.0, 0.0).astype(BF16)
    ub = jnp.dot(to_segments, u_ref[...].astype(BF16), preferred_element_type=F32).astype(BF16)
    for kb in range(n_kb):
        bu = jnp.dot(ub[:, kb * kw:(kb + 1) * kw], wb_ref[kb], preferred_element_type=F32)
        xs_ref[:, kb * sw:(kb + 1) * sw] = bu[:, :sw]
        xs_ref[:, half + kb * sw:half + (kb + 1) * sw] = bu[:, sw:]

    for c in range(half // strip):
        re = slice(c * strip, (c + 1) * strip)
        im = slice(half + c * strip, half + (c + 1) * strip)
        a_re = jnp.broadcast_to(pre_ref[0:1, re], (SUBLANE, strip))
        a_im = jnp.broadcast_to(pim_ref[0:1, re], (SUBLANE, strip))

        def local(j, carry):
            x_re, x_im = carry
            rows = pl.ds(pl.multiple_of(j * SUBLANE, SUBLANE), SUBLANE)
            n_re = a_re * x_re - a_im * x_im + xs_ref[rows, re]
            n_im = a_re * x_im + a_im * x_re + xs_ref[rows, im]
            xs_ref[rows, re] = n_re
            xs_ref[rows, im] = n_im
            return n_re, n_im

        zero = jnp.zeros((SUBLANE, strip), F32)
        e_re, e_im = lax.fori_loop(0, seg, local, (zero, zero), unroll=True)

        al_re, al_im = pre_ref[seg - 1:seg, re], pim_ref[seg - 1:seg, re]
        c_re, c_im = xc_ref[:, re], xc_ref[:, im]
        ins_re, ins_im = [], []
        for s in range(SUBLANE):
            ins_re.append(c_re)
            ins_im.append(c_im)
            c_re, c_im = (al_re * c_re - al_im * c_im + e_re[s:s + 1, :],
                          al_re * c_im + al_im * c_re + e_im[s:s + 1, :])
        xc_ref[:, re] = c_re
        xc_ref[:, im] = c_im
        in_re = jnp.concatenate(ins_re, axis=0)
        in_im = jnp.concatenate(ins_im, axis=0)

        def fix(j, carry):
            rows = pl.ds(pl.multiple_of(j * SUBLANE, SUBLANE), SUBLANE)
            w_re, w_im = pre_ref[pl.ds(j, 1), re], pim_ref[pl.ds(j, 1), re]
            xs_ref[rows, re] = xs_ref[rows, re] + (w_re * in_re - w_im * in_im)
            xs_ref[rows, im] = xs_ref[rows, im] + (w_re * in_im + w_im * in_re)
            return carry

        lax.fori_loop(0, seg, fix, 0, unroll=True)

    xl_ref[0] = xc_ref[...]
    for kb in range(n_kb):
        xb = jnp.concatenate([xs_ref[:, kb * sw:(kb + 1) * sw],
                              xs_ref[:, half + kb * sw:half + (kb + 1) * sw]], axis=1).astype(BF16)
        up_ref[:, kb * kw:(kb + 1) * kw] = jnp.dot(xb, wc_ref[kb], preferred_element_type=F32)
    y = _permute_rows(to_time, up_ref[...])
    z_ref[...] = jax.nn.gelu(y + d_ref[...] * u_ref[...])


def s5_scan(u_arr, row0, col0, n_seq, seq, x0, ld_re, ld_im, wb, wc, d_skip):
    n_kb, kw, two_sw = wb.shape
    width = n_kb * kw
    half = n_kb * two_sw // 2
    tb = _tile(seq, 128)
    nt = seq // tb
    seg = tb // SUBLANE
    assert row0 % tb == 0 and tb % SUBLANE == 0
    steps = jnp.arange(1, seg + 1, dtype=F32)[:, None]
    mag = jnp.exp(steps * ld_re)
    pow_re, pow_im = mag * jnp.cos(steps * ld_im), mag * jnp.sin(steps * ld_im)
    z, xl = pl.pallas_call(
        functools.partial(_s5_kernel, n_kb=n_kb, strip=min(1024, half)),
        grid=(n_seq, nt),
        in_specs=[pl.BlockSpec((tb, width), lambda s, t: (row0 // tb + s * nt + t, col0)),
                  pl.BlockSpec((1, 1, 2 * half), lambda s, t: (s, 0, 0)),
                  pl.BlockSpec((seg, half), lambda s, t: (0, 0)),
                  pl.BlockSpec((seg, half), lambda s, t: (0, 0)),
                  pl.BlockSpec(wb.shape, lambda s, t: (0, 0, 0)),
                  pl.BlockSpec(wc.shape, lambda s, t: (0, 0, 0)),
                  pl.BlockSpec((1, width), lambda s, t: (0, 0))],
        out_specs=[pl.BlockSpec((tb, width), lambda s, t: (s * nt + t, 0)),
                   pl.BlockSpec((1, 1, 2 * half), lambda s, t: (s, 0, 0))],
        out_shape=[jax.ShapeDtypeStruct((n_seq * seq, width), F32),
                   jax.ShapeDtypeStruct((n_seq, 1, 2 * half), F32)],
        scratch_shapes=[pltpu.VMEM((tb, width), F32), pltpu.VMEM((tb, 2 * half), F32),
                        pltpu.VMEM((1, 2 * half), F32)],
        compiler_params=_cparams("parallel", "arbitrary"),
        name="s5_scan",
    )(u_arr, x0, pow_re, pow_im, wb, wc, d_skip.reshape(1, -1))
    return z, xl.reshape(n_seq, 2 * half)


def s5_params(a_re, a_im, log_dt, b_re, b_im, c_re, c_im, kw):
    g, n, gc = b_re.shape
    gpb = kw // gc
    n_kb = g // gpb
    dt = jnp.exp(log_dt)[:, None]
    ld_re, ld_im = a_re * dt, a_im * dt
    e = jnp.exp(ld_re)
    abar_re, abar_im = e * jnp.cos(ld_im), e * jnp.sin(ld_im)
    den = a_re * a_re + a_im * a_im
    q_re = ((abar_re - 1.0) * a_re + abar_im * a_im) / den
    q_im = (abar_im * a_re - (abar_re - 1.0) * a_im) / den
    bb_re = q_re[..., None] * b_re - q_im[..., None] * b_im
    bb_im = q_re[..., None] * b_im + q_im[..., None] * b_re
    eye = jnp.eye(gpb, dtype=F32)

    def pack_b(bb):
        bb = bb.reshape(n_kb, gpb, n, gc)
        return jnp.einsum('kgnc,gh->kgchn', bb, eye).reshape(n_kb, gpb * gc, gpb * n)

    def pack_c(cc):
        cc = cc.reshape(n_kb, gpb, gc, n)
        return jnp.einsum('kgcn,gh->kgnhc', cc, eye).reshape(n_kb, gpb * n, gpb * gc)

    wb = jnp.concatenate([pack_b(bb_re), pack_b(bb_im)], axis=2).astype(BF16)
    wc = jnp.concatenate([pack_c(c_re), pack_c(-c_im)], axis=1).astype(BF16)
    return ld_re.reshape(1, g * n), ld_im.reshape(1, g * n), wb, wc


def _pad_cols(w, n):
    return jnp.pad(w, ((0, 0), (0, n - w.shape[1])))


def _layer(x, h1, cfg, cos, sin, st, p, g_after):
    seq, bsz, sq, past = cfg["seq"], cfg["bsz"], cfg["sq"], cfg["past"]
    t_all, d = x.shape
    n_s = bsz * sq
    width = cfg["mix_w"]
    n_heads, c_heads = cfg["a_heads"], cfg["c_heads"]

    layer = st["layer"]
    ff = p["ffn_w_down"].shape[2]

    def ffn(x, h, idx, g_post, g_next):
        hid = swiglu_up(h, p["ffn_w_up"], (layer, idx))
        y = matmul(hid, p["ffn_w_down"], b_sel=(layer, idx), out_dtype=F32, tm=512, tn=512,
                   tk=ff if ff <= 3 * 4096 else None, name="ffn_down")
        return add_rmsnorm(x, y, g_post, 0.5, g_next)

    g = p["norm_g"]
    x, h2 = ffn(x, h1, 0, g[1], g[2])

    a_cols = cfg["q_lora"] + cfg["kv_lora"] + cfg["a_rope"]
    cpad = _round_up(cfg["q_lora"] + cfg["kv_lora"] + LANE, width)
    c_cols = 3 * c_heads * LANE
    o_b, o_c, o_d, o_g = a_cols, a_cols + width, a_cols + width + c_cols, a_cols + 2 * width + c_cols
    w_in = p["w_in"]
    w_segf = jnp.concatenate([_pad_cols(w_in[:, :a_cols], cpad), w_in[:, o_b:o_c], w_in[:, o_d:o_g]],
                             axis=1).astype(BF16)
    segf = matmul(h2, w_segf, out_dtype=F32, name="w_in_f32")
    qkvc = matmul(h2, w_in[:, o_c:o_d].astype(BF16), out_dtype=F32, name="w_in_qkv")
    gates = matmul(h2, w_in[:, o_g:].astype(BF16), out_dtype=BF16, epilogue=_ep_sigmoid, name="w_in_gates")
    ub_col = cpad // width
    ud_col = ub_col + 1

    nope = LANE
    w_uq = p["mla_w_uq"].reshape(cfg["q_lora"], n_heads, nope + cfg["a_rope"])
    w_uq = jnp.pad(w_uq, ((0, 0), (0, 0), (0, 2 * LANE - nope - cfg["a_rope"])))
    w_uq = w_uq.reshape(cfg["q_lora"], n_heads * 2 * LANE).astype(BF16)
    w_ukv = p["mla_w_ukv"]
    cq, ckv, ckv_b, kr, kr_b = mla_prep(segf, p["mla_q_norm"], p["mla_kv_norm"], cos, sin, cpad)
    tm_q = _tile(t_all, 1024)
    q = matmul(cq, w_uq, out_dtype=BF16, tm=tm_q, tn=1024, name="mla_q",
               epilogue=functools.partial(_ep_rope_q, scale=(nope + cfg["a_rope"]) ** -0.5 * math.log2(math.e),
                                          half=cfg["a_rope"] // 2),
               extra=[(cos, (tm_q, LANE), lambda i, j: (i, 0)), (sin, (tm_q, LANE), lambda i, j: (i, 0))])
    tm_kv = _tile(seq, 1024)
    kv_p = matmul(ckv_b, w_ukv, b_sel=(layer,), out_dtype=BF16, m=seq, tm=tm_kv, widen=2, name="mla_kv_p",
                  epilogue=_ep_pack_kv,
                  extra=[(kr_b, (tm_kv, LANE), lambda i, j: (i, 0))])
    oa_p = mla_flash(q, kv_p, seq, n_heads)
    n_keys = past + sq
    kp = _round_up(n_keys, LANE)
    ckv_all = jnp.concatenate([st["ckv"].astype(BF16), ckv_b[seq:].reshape(bsz, sq, -1),
                               jnp.zeros((bsz, kp - n_keys, cfg["kv_lora"]), BF16)], axis=1)
    kr_all = jnp.concatenate([_pad_cols(st["kr"].reshape(bsz * past, -1), LANE).reshape(bsz, past, LANE).astype(BF16),
                              kr_b[seq:].reshape(bsz, sq, LANE), jnp.zeros((bsz, kp - n_keys, LANE), BF16)], axis=1)
    kv_s = matmul(ckv_all.reshape(bsz * kp, -1), w_ukv, b_sel=(layer,), out_dtype=BF16, name="mla_kv_s")
    oa_s = mla_sample(q, kv_s, kr_all.reshape(bsz * kp, LANE), seq, bsz, sq, n_heads, n_keys, past)
    o_a = jnp.concatenate([oa_p, oa_s], axis=0)

    lru = (p["lru_conv_w"], p["lru_conv_b"], p["lru_w_a"].astype(BF16), p["lru_b_a"],
           p["lru_w_x"].astype(BF16), p["lru_b_x"], p["lru_lambda"])
    n_conv = p["lru_conv_w"].shape[0] - 1
    ob_p, hl_p = rglru(segf, 0, ub_col, 1, seq, jnp.zeros((1, width), F32), jnp.zeros((1, n_conv, width), F32), *lru)
    ob_s, hl_s = rglru(segf, seq, ub_col, bsz, sq, st["h"], st["conv"], *lru)
    o_bb = jnp.concatenate([ob_p, ob_s], axis=0)
    conv_p = segf[seq - n_conv:seq, cpad:cpad + width][None]
    conv_s = segf[seq:, cpad:cpad + width].reshape(bsz, sq, width)[:, sq - n_conv:]

    win = BAND_CHUNKS * CHUNK
    tq_c = win
    assert seq % tq_c == 0
    bias_pp, bias_po = band_bias_tables(p["band_rel_bias"], tq_c, win, tq_c, tq_c)
    hps_p = 2 if c_heads % 2 == 0 else 1
    kcol, vcol = c_heads // hps_p, 2 * c_heads // hps_p
    prev_map = lambda col: (lambda i, g: (jnp.maximum(i - 1, 0), col + g))
    own_map = lambda col: (lambda i, g: (i, col + g))
    oc_p = band_attention(qkvc, 0, 0, qkvc, prev_map(kcol), qkvc, prev_map(vcol), qkvc, own_map(kcol),
                          qkvc, own_map(vcol), bias_pp, bias_po, n_heads=c_heads, hps=hps_p, n_blocks=seq // tq_c,
                          tq=tq_c, prev=win, own=tq_c, mask_first_prev=True)
    own_s = LANE
    k_new = qkvc[seq:, c_heads * LANE:2 * c_heads * LANE].reshape(bsz, sq, c_heads * LANE)
    v_new = qkvc[seq:, 2 * c_heads * LANE:].reshape(bsz, sq, c_heads * LANE)
    pad_own = lambda a: jnp.pad(a, ((0, 0), (0, own_s - sq), (0, 0))).reshape(bsz * own_s, c_heads * LANE)
    bias_sp, bias_so = band_bias_tables(p["band_rel_bias"], sq, win, own_s, sq)
    cache_k, cache_v, cache_blk0 = st["bk_all"], st["bv_all"], st["layer"] * bsz
    batch_map = lambda i, h: (i, h)
    cache_map = lambda i, h: (cache_blk0 + i, h)
    oc_s = band_attention(qkvc, seq, 0, cache_k, cache_map, cache_v, cache_map, pad_own(k_new), batch_map,
                          pad_own(v_new), batch_map, bias_sp, bias_so, n_heads=c_heads, hps=c_heads, n_blocks=bsz,
                          tq=sq, prev=win, own=own_s, mask_first_prev=False)
    o_cc = jnp.concatenate([oc_p, oc_s], axis=0)
    keep = min(win, seq)
    bk_p = qkvc[seq - keep:seq, c_heads * LANE:2 * c_heads * LANE].reshape(1, keep, c_heads, LANE)
    bv_p = qkvc[seq - keep:seq, 2 * c_heads * LANE:].reshape(1, keep, c_heads, LANE)
    bk_s = k_new.reshape(bsz, sq, c_heads, LANE)
    bv_s = v_new.reshape(bsz, sq, c_heads, LANE)

    n_grp, n_state = p["s5_a_re"].shape
    abar_re, abar_im, wb, wc = s5_params(p["s5_a_re"], p["s5_a_im"], p["s5_log_dt"], p["s5_b_re"], p["s5_b_im"],
                                         p["s5_c_re"], p["s5_c_im"], LANE)
    pack0 = lambda re, im: jnp.concatenate([re.reshape(-1, 1, n_grp * n_state), im.reshape(-1, 1, n_grp * n_state)],
                                           axis=2)
    zeros0 = jnp.zeros((1, n_grp, n_state), F32)
    z_p, xl_p = s5_scan(segf, 0, ud_col, 1, seq, pack0(zeros0, zeros0), abar_re, abar_im, wb, wc, p["s5_d"])
    z_s, xl_s = s5_scan(segf, seq, ud_col, bsz, sq, pack0(st["s5re"], st["s5im"]), abar_re, abar_im, wb, wc,
                        p["s5_d"])
    z = jnp.concatenate([z_p, z_s], axis=0)
    tm_g = _tile(t_all, 1024)
    tn_g = _tile(width, 1024)
    o_dd = matmul(z.astype(BF16), p["s5_w_glu"], b_sel=(layer,), out_dtype=BF16, tm=tm_g, tn=tn_g, epilogue=_ep_glu,
                  extra=[(z, (tm_g, tn_g), lambda i, j: (i, j)),
                         (p["s5_b_glu"].reshape(1, -1), (1, tn_g), lambda i, j: (0, j))], name="s5_glu")
    unpack = lambda xl, part: xl[:, part * n_grp * n_state:(part + 1) * n_grp * n_state].reshape(-1, n_grp, n_state)

    merged = merge_branches((o_a, o_bb, o_cc, o_dd), p["w_branch"], layer, gates)
    y = matmul(merged, p["w_out"], b_sel=(layer,), out_dtype=F32, name="w_out")
    x, h3 = add_rmsnorm(x, y, g[3], 1.0, g[4])
    if g_after is None:
        x, h_after = ffn(x, h3, 1, g[5], None), None
    else:
        x, h_after = ffn(x, h3, 1, g[5], g_after)

    new_p = (ckv[:seq][None], kr[:seq, :cfg["a_rope"]][None], hl_p, conv_p, bk_p, bv_p, unpack(xl_p, 0),
             unpack(xl_p, 1))
    new_s = (ckv[seq:].reshape(bsz, sq, -1), kr[seq:, :cfg["a_rope"]].reshape(bsz, sq, -1), hl_s, conv_s, bk_s, bv_s,
             unpack(xl_s, 0), unpack(xl_s, 1))
    return x, h_after, new_p, new_s


def kernel(x_prompt, x_sample, cache_mla_ckv, cache_mla_krope, state_lru_h, state_lru_conv, cache_band_k, cache_band_v, state_s5_re, state_s5_im, norm_g, ffn_w_up, ffn_w_down, w_in, mla_q_norm, mla_kv_norm, mla_w_uq, mla_w_ukv, lru_conv_w, lru_conv_b, lru_w_a, lru_b_a, lru_w_x, lru_b_x, lru_lambda, band_rel_bias, s5_a_re, s5_a_im, s5_log_dt, s5_b_re, s5_b_im, s5_c_re, s5_c_im, s5_d, s5_w_glu, s5_b_glu, w_branch, w_out):
    depth = norm_g.shape[0]
    n_p, seq, d = x_prompt.shape
    bsz, sq, _ = x_sample.shape
    past = cache_mla_ckv.shape[2]
    a_rope = cache_mla_krope.shape[3]
    mix_w = w_branch.shape[2]
    a_heads = mix_w // LANE
    c_heads = cache_band_k.shape[3]
    assert n_p == 1 and a_rope == 2 * ROPE_HALF and cache_band_k.shape[4] == LANE
    assert mla_w_uq.shape[2] == a_heads * (LANE + a_rope) and mla_w_ukv.shape[2] == a_heads * 2 * LANE
    assert past % CHUNK == 0 and cache_band_k.shape[2] == BAND_CHUNKS * CHUNK and past >= BAND_CHUNKS * CHUNK
    assert sq <= CHUNK and seq % (BAND_CHUNKS * CHUNK) == 0 and sq >= lru_conv_w.shape[1] - 1
    cfg = dict(seq=seq, bsz=bsz, sq=sq, past=past, mix_w=mix_w, a_heads=a_heads, c_heads=c_heads,
               q_lora=mla_q_norm.shape[1], kv_lora=mla_kv_norm.shape[1], a_rope=a_rope)

    pos = np.concatenate([np.arange(seq), np.tile(past + np.arange(sq), bsz)]).astype(np.float32)
    half = a_rope // 2
    inv = ROPE_THETA ** (-jnp.arange(half, dtype=F32) / half)
    ang = jnp.asarray(pos)[:, None] * inv
    zpad = jnp.zeros((pos.shape[0], LANE - a_rope), F32)
    cos = jnp.concatenate([jnp.cos(ang), jnp.cos(ang), zpad], axis=1)
    sin = jnp.concatenate([-jnp.sin(ang), jnp.sin(ang), zpad], axis=1)

    x = jnp.concatenate([x_prompt.reshape(seq, d), x_sample.reshape(bsz * sq, d)], axis=0)
    h = rmsnorm(x, norm_g[0, 0])
    stacked = dict(ffn_w_up=ffn_w_up.astype(BF16), ffn_w_down=ffn_w_down.astype(BF16),
                   mla_w_ukv=mla_w_ukv.astype(BF16), s5_w_glu=s5_w_glu.astype(BF16),
                   w_branch=w_branch.astype(BF16), w_out=w_out.astype(BF16))
    new_p, new_s = [], []
    for l in range(depth):
        st = dict(ckv=cache_mla_ckv[l], kr=cache_mla_krope[l], h=state_lru_h[l], conv=state_lru_conv[l],
                  bk_all=cache_band_k.reshape(-1, c_heads * LANE), bv_all=cache_band_v.reshape(-1, c_heads * LANE),
                  layer=l, s5re=state_s5_re[l], s5im=state_s5_im[l])
        p = dict(stacked, norm_g=norm_g[l], w_in=w_in[l],
                 mla_q_norm=mla_q_norm[l], mla_kv_norm=mla_kv_norm[l], mla_w_uq=mla_w_uq[l],
                 lru_conv_w=lru_conv_w[l], lru_conv_b=lru_conv_b[l], lru_w_a=lru_w_a[l], lru_b_a=lru_b_a[l],
                 lru_w_x=lru_w_x[l], lru_b_x=lru_b_x[l], lru_lambda=lru_lambda[l], band_rel_bias=band_rel_bias[l],
                 s5_a_re=s5_a_re[l], s5_a_im=s5_a_im[l], s5_log_dt=s5_log_dt[l], s5_b_re=s5_b_re[l],
                 s5_b_im=s5_b_im[l], s5_c_re=s5_c_re[l], s5_c_im=s5_c_im[l], s5_d=s5_d[l], s5_b_glu=s5_b_glu[l])
        x, h, st_p, st_s = _layer(x, h, cfg, cos, sin, st, p, norm_g[l + 1, 0] if l + 1 < depth else None)
        new_p.append(st_p)
        new_s.append(st_s)

    outs = [x[:seq].reshape(1, seq, d), x[seq:].reshape(bsz, sq, d)]
    for i in range(8):
        outs.append(jnp.stack([s[i] for s in new_p], axis=0))
        outs.append(jnp.stack([s[i] for s in new_s], axis=0))
    return tuple(outs)
```

```python
import functools
import math

import jax
import jax.numpy as jnp
import numpy as np
from jax import lax
from jax.experimental import pallas as pl
from jax.experimental.pallas import tpu as pltpu

F32 = jnp.float32
BF16 = jnp.bfloat16

CHUNK = 64
BAND_CHUNKS = 8
EPS = 1e-6
ROPE_THETA = 10000.0
LRU_C = 8.0
ROPE_HALF = 32
MASK_VALUE = -1e30
LANE = 128
SUBLANE = 8
VMEM_LIMIT = 56 * 1024 * 1024


def _cparams(*sem):
    return pltpu.CompilerParams(dimension_semantics=sem, vmem_limit_bytes=VMEM_LIMIT)


def _round_up(x, m):
    return (x + m - 1) // m * m


def _tile(dim, pref):
    if dim <= pref:
        return dim
    t = pref
    while dim % t:
        t //= 2
    assert t >= SUBLANE, (dim, pref)
    return t


def _rmsnorm_kernel(x_ref, g_ref, o_ref):
    x = x_ref[...]
    ms = jnp.mean(x * x, axis=-1, keepdims=True)
    o_ref[...] = (x * lax.rsqrt(ms + EPS) * g_ref[...]).astype(o_ref.dtype)


def rmsnorm(x, g, out_dtype=BF16):
    t, d = x.shape
    tm = _tile(t, 256)
    return pl.pallas_call(
        _rmsnorm_kernel,
        grid=(t // tm,),
        in_specs=[pl.BlockSpec((tm, d), lambda i: (i, 0)), pl.BlockSpec((1, d), lambda i: (0, 0))],
        out_specs=pl.BlockSpec((tm, d), lambda i: (i, 0)),
        out_shape=jax.ShapeDtypeStruct((t, d), out_dtype),
        compiler_params=_cparams("parallel"),
        name="rmsnorm",
    )(x, g.reshape(1, d))


def _add_rmsnorm_kernel(x_ref, y_ref, g_ref, *rest, scale, with_next):
    y = y_ref[...]
    ms = jnp.mean(y * y, axis=-1, keepdims=True)
    x = x_ref[...] + scale * (y * lax.rsqrt(ms + EPS) * g_ref[...])
    if not with_next:
        rest[0][...] = x
        return
    gn_ref, o_ref, h_ref = rest
    o_ref[...] = x
    ms = jnp.mean(x * x, axis=-1, keepdims=True)
    h_ref[...] = (x * lax.rsqrt(ms + EPS) * gn_ref[...]).astype(h_ref.dtype)


def add_rmsnorm(x, y, g, scale, g_next=None):
    t, d = x.shape
    tm = _tile(t, 256)
    row = pl.BlockSpec((tm, d), lambda i: (i, 0))
    vec = pl.BlockSpec((1, d), lambda i: (0, 0))
    with_next = g_next is not None
    return pl.pallas_call(
        functools.partial(_add_rmsnorm_kernel, scale=scale, with_next=with_next),
        grid=(t // tm,),
        in_specs=[row, row, vec] + ([vec] if with_next else []),
        out_specs=[row, row] if with_next else row,
        out_shape=([jax.ShapeDtypeStruct((t, d), F32), jax.ShapeDtypeStruct((t, d), BF16)] if with_next
                   else jax.ShapeDtypeStruct((t, d), F32)),
        compiler_params=_cparams("parallel"),
        name="add_rmsnorm",
    )(x, y, g.reshape(1, d), *([g_next.reshape(1, d)] if with_next else []))


def _k_tile(k):
    if k <= 4096:
        return k
    for parts in range(2, k // LANE + 1):
        if k % parts == 0 and (k // parts) % LANE == 0 and k // parts <= 4096:
            return k // parts
    raise ValueError(k)


def _mm_kernel(*refs, nk, n_extra, epilogue):
    a_ref, b_ref = refs[0], refs[1]
    extra = refs[2:2 + n_extra]
    o_ref = refs[2 + n_extra]
    if nk == 1:
        acc = jnp.dot(a_ref[...], b_ref[...], preferred_element_type=F32)
        o_ref[...] = epilogue(acc, *extra).astype(o_ref.dtype)
        return
    acc_ref = refs[3 + n_extra]
    k = pl.program_id(2)

    @pl.when(k == 0)
    def _():
        acc_ref[...] = jnp.zeros_like(acc_ref)

    acc_ref[...] += jnp.dot(a_ref[...], b_ref[...], preferred_element_type=F32)

    @pl.when(k == nk - 1)
    def _():
        o_ref[...] = epilogue(acc_ref[...], *extra).astype(o_ref.dtype)


def _ep_none(acc):
    return acc


def _ep_sigmoid(acc):
    return jax.nn.sigmoid(acc)


def _ep_glu(acc, z_ref, b_ref):
    z = z_ref[...]
    return z * jax.nn.sigmoid(acc + b_ref[...])


def _ep_pack_kv(acc, kr_ref):
    kr = kr_ref[...].astype(F32)
    ones = jnp.ones_like(kr)
    outs = []
    for h in range(acc.shape[1] // (2 * LANE)):
        outs += [acc[:, h * 2 * LANE:h * 2 * LANE + LANE], kr, acc[:, h * 2 * LANE + LANE:(h + 1) * 2 * LANE], ones]
    return jnp.concatenate(outs, axis=1)


def _ep_rope_q(acc, cos_ref, sin_ref, *, scale, half):
    cos, sin = cos_ref[...], sin_ref[...]
    lane = lax.broadcasted_iota(jnp.int32, cos.shape, 1)
    outs = []
    for h in range(acc.shape[1] // (2 * LANE)):
        nope = acc[:, h * 2 * LANE:h * 2 * LANE + LANE]
        r = acc[:, h * 2 * LANE + LANE:(h + 1) * 2 * LANE]
        swapped = jnp.where(lane < half, pltpu.roll(r, LANE - half, axis=1), pltpu.roll(r, half, axis=1))
        outs += [nope * scale, (r * cos + swapped * sin) * scale]
    return jnp.concatenate(outs, axis=1)


def matmul(a, b, *, out_dtype, m=None, tm=1024, tn=1024, tk=None, b_sel=(), epilogue=_ep_none, extra=(), widen=1,
           name="matmul"):
    m = a.shape[0] if m is None else m
    k, n = b.shape[-2:]
    assert a.shape[1] == k and b.ndim == 2 + len(b_sel)
    tm, tn, tk = _tile(m, tm), _tile(n, tn), (_k_tile(k) if tk is None else tk)
    nk = k // tk
    lead = (None,) * len(b_sel)
    if nk == 1:
        grid = (m // tm, n // tn)
        wrap = lambda f: f
        a_spec = pl.BlockSpec((tm, k), lambda i, j: (i, 0))
        b_spec = pl.BlockSpec(lead + (k, tn), lambda i, j: (*b_sel, 0, j))
        o_spec = pl.BlockSpec((tm, tn * widen), lambda i, j: (i, j))
        scratch = []
        sem = ("parallel", "parallel")
    else:
        grid = (m // tm, n // tn, nk)
        wrap = lambda f: (lambda i, j, kk: f(i, j))
        a_spec = pl.BlockSpec((tm, tk), lambda i, j, kk: (i, kk))
        b_spec = pl.BlockSpec(lead + (tk, tn), lambda i, j, kk: (*b_sel, kk, j))
        o_spec = pl.BlockSpec((tm, tn * widen), lambda i, j, kk: (i, j))
        scratch = [pltpu.VMEM((tm, tn), F32)]
        sem = ("parallel", "parallel", "arbitrary")
    extra_specs = [pl.BlockSpec(bs, wrap(im)) for _, bs, im in extra]
    return pl.pallas_call(
        functools.partial(_mm_kernel, nk=nk, n_extra=len(extra), epilogue=epilogue),
        grid=grid,
        in_specs=[a_spec, b_spec] + extra_specs,
        out_specs=o_spec,
        out_shape=jax.ShapeDtypeStruct((m, n * widen), out_dtype),
        scratch_shapes=scratch,
        compiler_params=_cparams(*sem),
        name=name,
    )(a, b, *[e[0] for e in extra])


def _swiglu_kernel(a_ref, wg_ref, wu_ref, o_ref):
    a = a_ref[...]
    gate = jnp.dot(a, wg_ref[...], preferred_element_type=F32)
    up = jnp.dot(a, wu_ref[...], preferred_element_type=F32)
    o_ref[...] = (jax.nn.silu(gate) * up).astype(o_ref.dtype)


def swiglu_up(a, w_up, sel):
    t, d = a.shape
    ff = w_up.shape[-1] // 2
    cap = (36 * 1024 * 1024) // (4 * d)
    tall = [r for r in range(LANE, cap + 1, LANE) if t % r == 0]
    tm, tn = (tall[-1] if tall else _tile(t, 1024)), _tile(ff, 512)
    nj = ff // tn
    lead = (None,) * len(sel)
    return pl.pallas_call(
        _swiglu_kernel,
        grid=(t // tm, nj),
        in_specs=[pl.BlockSpec((tm, d), lambda i, j: (i, 0)),
                  pl.BlockSpec(lead + (d, tn), lambda i, j: (*sel, 0, j)),
                  pl.BlockSpec(lead + (d, tn), lambda i, j: (*sel, 0, j + nj))],
        out_specs=pl.BlockSpec((tm, tn), lambda i, j: (i, j)),
        out_shape=jax.ShapeDtypeStruct((t, ff), BF16),
        compiler_params=_cparams("parallel", "parallel"),
        name="swiglu_up",
    )(a, w_up, w_up)


def _merge_kernel(oa_ref, ob_ref, oc_ref, od_ref, wb_ref, ga_ref, gb_ref, gc_ref, gd_ref, o_ref):
    total = None
    for i, (o_r, g_r) in enumerate(((oa_ref, ga_ref), (ob_ref, gb_ref), (oc_ref, gc_ref), (od_ref, gd_ref))):
        part = g_r[...].astype(F32) * jnp.dot(o_r[...], wb_ref[i], preferred_element_type=F32)
        total = part if total is None else total + part
    o_ref[...] = total.astype(o_ref.dtype)


def merge_branches(outs, w_branch, layer, gates):
    t, w = outs[0].shape
    d = w_branch.shape[3]
    tm, tn = _tile(t, 512), _tile(d, 512)
    nj = d // tn
    o_spec = pl.BlockSpec((tm, w), lambda i, j: (i, 0))
    g_specs = [pl.BlockSpec((tm, tn), functools.partial(lambda i, j, b: (i, j + b * nj), b=b)) for b in range(4)]
    return pl.pallas_call(
        _merge_kernel,
        grid=(t // tm, nj),
        in_specs=[o_spec] * 4 + [pl.BlockSpec((None, 4, w, tn), lambda i, j: (layer, 0, 0, j))] + g_specs,
        out_specs=pl.BlockSpec((tm, tn), lambda i, j: (i, j)),
        out_shape=jax.ShapeDtypeStruct((t, d), BF16),
        compiler_params=_cparams("parallel", "parallel"),
        name="merge_branches",
    )(*outs, w_branch, gates, gates, gates, gates)


def _mla_prep_kernel(c_ref, qn_ref, kvn_ref, cos_ref, sin_ref, cq_ref, ckv_ref, ckvb_ref, kr_ref, krb_ref,
                     *, q_lora, kv_lora, half):
    c_q = c_ref[:, :q_lora]
    ms = jnp.mean(c_q * c_q, axis=-1, keepdims=True)
    cq_ref[...] = (c_q * lax.rsqrt(ms + EPS) * qn_ref[...]).astype(cq_ref.dtype)
    c_kv = c_ref[:, q_lora:q_lora + kv_lora]
    ms = jnp.mean(c_kv * c_kv, axis=-1, keepdims=True)
    ckv = c_kv * lax.rsqrt(ms + EPS) * kvn_ref[...]
    ckv_ref[...] = ckv
    ckvb_ref[...] = ckv.astype(ckvb_ref.dtype)
    r = c_ref[:, q_lora + kv_lora:q_lora + kv_lora + LANE]
    lane = lax.broadcasted_iota(jnp.int32, r.shape, 1)
    swapped = jnp.where(lane < half, pltpu.roll(r, LANE - half, axis=1), pltpu.roll(r, half, axis=1))
    kr = r * cos_ref[...] + swapped * sin_ref[...]
    kr_ref[...] = kr
    krb_ref[...] = kr.astype(krb_ref.dtype)


def mla_prep(segf, q_norm, kv_norm, cos, sin, cpad):
    t = segf.shape[0]
    q_lora, kv_lora = q_norm.shape[0], kv_norm.shape[0]
    tm = _tile(t, 256)
    row = lambda w: pl.BlockSpec((tm, w), lambda i: (i, 0))
    const = lambda w: pl.BlockSpec((1, w), lambda i: (0, 0))
    return pl.pallas_call(
        functools.partial(_mla_prep_kernel, q_lora=q_lora, kv_lora=kv_lora, half=ROPE_HALF),
        grid=(t // tm,),
        in_specs=[row(cpad), const(q_lora), const(kv_lora), row(LANE), row(LANE)],
        out_specs=[row(q_lora), row(kv_lora), row(kv_lora), row(LANE), row(LANE)],
        out_shape=[jax.ShapeDtypeStruct((t, q_lora), BF16), jax.ShapeDtypeStruct((t, kv_lora), F32),
                   jax.ShapeDtypeStruct((t, kv_lora), BF16), jax.ShapeDtypeStruct((t, LANE), F32),
                   jax.ShapeDtypeStruct((t, LANE), BF16)],
        compiler_params=_cparams("parallel"),
        name="mla_prep",
    )(segf, q_norm.reshape(1, -1), kv_norm.reshape(1, -1), cos, sin)


def _mla_flash_kernel(qi_ref, ki_ref, q_ref, kv_ref, o_ref, m_ref, acc_ref, *, heads, tq, tk):
    s_id = pl.program_id(1)
    qi, ki = qi_ref[s_id], ki_ref[s_id]
    hw = 2 * LANE

    @pl.when(ki == 0)
    def _():
        m_ref[...] = jnp.full_like(m_ref, -jnp.inf)
        acc_ref[...] = jnp.zeros_like(acc_ref)

    def step(masked):
        if masked:
            rc = lax.broadcasted_iota(jnp.int32, (tq, tk), 0) // CHUNK
            cc = lax.broadcasted_iota(jnp.int32, (tq, tk), 1) // CHUNK
            visible = cc <= rc
        for h in range(heads):
            q = q_ref[:, h * hw:(h + 1) * hw]
            k = kv_ref[:, 2 * h * hw:(2 * h + 1) * hw]
            v1 = kv_ref[:, (2 * h + 1) * hw:(2 * h + 2) * hw]
            s = lax.dot_general(q, k, (((1,), (1,)), ((), ())), preferred_element_type=F32)
            if masked:
                s = jnp.where(visible, s, MASK_VALUE)
            m_prev = m_ref[h]
            m_next = jnp.maximum(m_prev, jnp.max(s, axis=1)[:, None])
            p = jnp.exp2(s - jnp.tile(m_next, (1, tk // LANE)))
            alpha = jnp.exp2(m_prev - m_next)
            m_ref[h] = m_next
            pv = jnp.dot(p.astype(BF16), v1, preferred_element_type=F32)
            acc_ref[:, h * hw:(h + 1) * hw] = jnp.tile(alpha, (1, 2)) * acc_ref[:, h * hw:(h + 1) * hw] + pv

    @pl.when(ki < qi)
    def _():
        step(False)

    @pl.when(ki == qi)
    def _():
        step(True)
        for h in range(heads):
            o_ref[:, h * LANE:(h + 1) * LANE] = (
                acc_ref[:, h * hw:h * hw + LANE] / acc_ref[:, h * hw + LANE:(h + 1) * hw]).astype(o_ref.dtype)


def mla_flash(q, kv, seq, n_heads, *, tile=1024, heads_per_step=8):
    tq = tk = _tile(seq, tile)
    hg = min(heads_per_step, n_heads)
    assert n_heads % hg == 0 and tq % CHUNK == 0
    nq = seq // tq
    pairs = [(i, j) for i in range(nq) for j in range(i + 1)]
    qi_arr = jnp.asarray(np.array([p[0] for p in pairs], np.int32))
    ki_arr = jnp.asarray(np.array([p[1] for p in pairs], np.int32))
    grid_spec = pltpu.PrefetchScalarGridSpec(
        num_scalar_prefetch=2,
        grid=(n_heads // hg, len(pairs)),
        in_specs=[pl.BlockSpec((tq, hg * 2 * LANE), lambda g, s, qi, ki: (qi[s], g)),
                  pl.BlockSpec((tk, hg * 4 * LANE), lambda g, s, qi, ki: (ki[s], g))],
        out_specs=pl.BlockSpec((tq, hg * LANE), lambda g, s, qi, ki: (qi[s], g)),
        scratch_shapes=[pltpu.VMEM((hg, tq, LANE), F32), pltpu.VMEM((tq, hg * 2 * LANE), F32)],
    )
    return pl.pallas_call(
        functools.partial(_mla_flash_kernel, heads=hg, tq=tq, tk=tk),
        grid_spec=grid_spec,
        out_shape=jax.ShapeDtypeStruct((seq, n_heads * LANE), BF16),
        compiler_params=_cparams("parallel", "arbitrary"),
        name="mla_flash",
    )(qi_arr, ki_arr, q, kv)


def _mla_sample_kernel(q_ref, kv_ref, kr_ref, o_ref, *, heads, n_keys, past):
    sq, kp = q_ref.shape[0], kv_ref.shape[0]
    kr = kr_ref[...]
    qpos = past + lax.broadcasted_iota(jnp.int32, (sq, kp), 0)
    kpos = lax.broadcasted_iota(jnp.int32, (sq, kp), 1)
    visible = (kpos < n_keys) & (kpos // CHUNK <= qpos // CHUNK)
    for h in range(heads):
        q = q_ref[:, h * 2 * LANE:(h + 1) * 2 * LANE]
        k = jnp.concatenate([kv_ref[:, h * 2 * LANE:h * 2 * LANE + LANE], kr], axis=1)
        v = kv_ref[:, h * 2 * LANE + LANE:(h + 1) * 2 * LANE]
        s = lax.dot_general(q, k, (((1,), (1,)), ((), ())), preferred_element_type=F32)
        s = jnp.where(visible, s, MASK_VALUE)
        p = jnp.exp2(s - jnp.max(s, axis=-1, keepdims=True))
        denom = jnp.sum(p, axis=-1, keepdims=True)
        pv = jnp.dot(p.astype(v.dtype), v, preferred_element_type=F32)
        o_ref[:, h * LANE:(h + 1) * LANE] = (pv / denom).astype(o_ref.dtype)


def mla_sample(q, kv, kr, row0, bsz, sq, n_heads, n_keys, past):
    kp = kv.shape[0] // bsz
    assert row0 % sq == 0
    return pl.pallas_call(
        functools.partial(_mla_sample_kernel, heads=n_heads, n_keys=n_keys, past=past),
        grid=(bsz,),
        in_specs=[pl.BlockSpec((sq, n_heads * 2 * LANE), lambda b: (row0 // sq + b, 0)),
                  pl.BlockSpec((kp, n_heads * 2 * LANE), lambda b: (b, 0)),
                  pl.BlockSpec((kp, LANE), lambda b: (b, 0))],
        out_specs=pl.BlockSpec((sq, n_heads * LANE), lambda b: (b, 0)),
        out_shape=jax.ShapeDtypeStruct((bsz * sq, n_heads * LANE), BF16),
        compiler_params=_cparams("parallel"),
        name="mla_sample",
    )(q, kv, kr)


def _band_kernel(q_ref, kp_ref, ko_ref, vp_ref, vo_ref, bp_ref, bo_ref, o_ref, *, heads, scale, mask_first_prev):
    dn = (((1,), (1,)), ((), ()))
    ones_p = jnp.ones((kp_ref.shape[0], LANE), BF16)
    ones_o = jnp.ones((ko_ref.shape[0], LANE), BF16)
    for h in range(heads):
        cs = slice(h * LANE, (h + 1) * LANE)
        q = (q_ref[:, cs] * scale).astype(BF16)
        sp = lax.dot_general(q, kp_ref[:, cs].astype(BF16), dn, preferred_element_type=F32) + bp_ref[h]
        so = lax.dot_general(q, ko_ref[:, cs].astype(BF16), dn, preferred_element_type=F32) + bo_ref[h]
        if mask_first_prev:
            sp = jnp.where(pl.program_id(1) > 0, sp, MASK_VALUE)
        m = jnp.maximum(jnp.max(sp, axis=-1, keepdims=True), jnp.max(so, axis=-1, keepdims=True))
        pp, po = jnp.exp2(sp - m), jnp.exp2(so - m)
        vp1 = jnp.concatenate([vp_ref[:, cs].astype(BF16), ones_p], axis=1)
        vo1 = jnp.concatenate([vo_ref[:, cs].astype(BF16), ones_o], axis=1)
        pv = (jnp.dot(pp.astype(BF16), vp1, preferred_element_type=F32)
              + jnp.dot(po.astype(BF16), vo1, preferred_element_type=F32))
        o_ref[:, cs] = (pv[:, :LANE] / pv[:, LANE:]).astype(o_ref.dtype)


def band_attention(q_arr, q_row0, q_col0, kp_arr, kp_map, vp_arr, vp_map, ko_arr, ko_map, vo_arr, vo_map,
                   bias_p, bias_o, *, n_heads, hps, n_blocks, tq, prev, own, mask_first_prev):
    hd = LANE
    w = hps * hd
    assert q_row0 % tq == 0 and n_heads % hps == 0
    return pl.pallas_call(
        functools.partial(_band_kernel, heads=hps, scale=hd ** -0.5 * math.log2(math.e),
                          mask_first_prev=mask_first_prev),
        grid=(n_heads // hps, n_blocks),
        in_specs=[pl.BlockSpec((tq, w), lambda g, i: (q_row0 // tq + i, q_col0 + g)),
                  pl.BlockSpec((prev, w), lambda g, i: kp_map(i, g)),
                  pl.BlockSpec((own, w), lambda g, i: ko_map(i, g)),
                  pl.BlockSpec((prev, w), lambda g, i: vp_map(i, g)),
                  pl.BlockSpec((own, w), lambda g, i: vo_map(i, g)),
                  pl.BlockSpec((hps, tq, prev), lambda g, i: (g, 0, 0)),
                  pl.BlockSpec((hps, tq, own), lambda g, i: (g, 0, 0))],
        out_specs=pl.BlockSpec((tq, w), lambda g, i: (i, g)),
        out_shape=jax.ShapeDtypeStruct((n_blocks * tq, n_heads * hd), BF16),
        compiler_params=_cparams("parallel", "arbitrary"),
        name="band_attention",
    )(q_arr, kp_arr, ko_arr, vp_arr, vo_arr, bias_p, bias_o)


def band_bias_tables(rel_bias, tq, prev, own, own_valid):
    n_heads = rel_bias.shape[0]
    clip = (rel_bias.shape[1] - 1) // 2
    qr = np.arange(tq)[:, None]

    def table(krel, valid):
        nk, k0 = krel.shape[1], int(krel[0, 0])
        period = nk + tq
        e = np.zeros(period, np.int64)
        e[:nk] = -np.arange(nk)
        e[nk + 1:] = np.arange(tq - 1, 0, -1)
        per_offset = jnp.take(rel_bias, jnp.asarray(np.clip(e - k0, -clip, clip) + clip), axis=1)
        toeplitz = jnp.tile(per_offset, (1, tq))[:, :tq * (period - 1)].reshape(n_heads, tq, period - 1)[:, :, :nk]
        dc = qr // CHUNK - np.floor_divide(krel, CHUNK)
        vis = (dc >= 0) & (dc <= BAND_CHUNKS) & valid
        return jnp.where(jnp.asarray(vis)[None], toeplitz * math.log2(math.e), MASK_VALUE)

    kp = np.arange(-prev, 0)[None, :]
    ko = np.arange(own)[None, :]
    return table(kp, np.ones_like(kp, bool)), table(ko, ko < own_valid)


def _rglru_kernel(u_ref, h0_ref, c0_ref, cw_ref, cb_ref, wa_ref, ba_ref, wx_ref, bx_ref, lam_ref,
                  o_ref, hl_ref, prev_ref, h_ref, a_ref, b_ref, *, n_blocks, bw):
    t = pl.program_id(1)
    tb, width = u_ref.shape

    @pl.when(t == 0)
    def _():
        prev_ref[...] = c0_ref[0]
        h_ref[...] = h0_ref[0]

    u = u_ref[...]
    cw = cw_ref[...]
    nw = 4
    xc = cb_ref[...] + cw[nw - 1:nw, :] * u
    for s in range(1, nw):
        xc = xc + cw[nw - 1 - s:nw - s, :] * pltpu.roll(u, s, axis=0)
    head = jnp.concatenate([prev_ref[...], u[:SUBLANE, :]], axis=0)
    xh = cb_ref[...] + cw[nw - 1:nw, :] * u[:SUBLANE, :]
    for s in range(1, nw):
        xh = xh + cw[nw - 1 - s:nw - s, :] * pltpu.roll(head, s, axis=0)[SUBLANE:, :]
    a_ref[:SUBLANE, :] = xh
    a_ref[SUBLANE:, :] = xc[SUBLANE:, :]
    xc = a_ref[...]
    prev_ref[...] = u[tb - SUBLANE:, :]

    neg_c_sp = -LRU_C * jax.nn.softplus(-lam_ref[...])
    for n in range(n_blocks):
        sl = slice(n * bw, (n + 1) * bw)
        xb = xc[:, sl].astype(BF16)
        r = jax.nn.sigmoid(jnp.dot(xb, wa_ref[n], preferred_element_type=F32) + ba_ref[:, sl])
        i = jax.nn.sigmoid(jnp.dot(xb, wx_ref[n], preferred_element_type=F32) + bx_ref[:, sl])
        log_a = neg_c_sp[:, sl] * r
        a = jnp.exp(log_a)
        a_ref[:, sl] = a
        b_ref[:, sl] = jnp.sqrt(-jnp.tanh(log_a) * (a * a + 1.0)) * (i * xc[:, sl])

    def row(j, h):
        h = a_ref[pl.ds(j, 1), :] * h + b_ref[pl.ds(j, 1), :]
        b_ref[pl.ds(j, 1), :] = h
        return h

    h = lax.fori_loop(0, tb, row, h_ref[...], unroll=8)
    h_ref[...] = h
    o_ref[...] = b_ref[...].astype(o_ref.dtype)
    hl_ref[0] = h


def rglru(u_arr, row0, col0, n_seq, seq, h0, conv0, conv_w, conv_b, w_a, b_a, w_x, b_x, lam):
    width = h0.shape[-1]
    n_blocks, bw = w_a.shape[0], w_a.shape[1]
    tb = _tile(seq, 256)
    nt = seq // tb
    assert row0 % tb == 0 and tb >= 2 * SUBLANE and conv_w.shape[0] == 4
    conv0p = jnp.concatenate([jnp.zeros((n_seq, SUBLANE - conv0.shape[1], width), F32), conv0], axis=1)
    cwp = jnp.concatenate([conv_w, jnp.zeros((SUBLANE - conv_w.shape[0], width), F32)], axis=0)
    vec = lambda: pl.BlockSpec((1, width), lambda s, t: (0, 0))
    wspec = lambda: pl.BlockSpec((n_blocks, bw, bw), lambda s, t: (0, 0, 0))
    out, hl = pl.pallas_call(
        functools.partial(_rglru_kernel, n_blocks=n_blocks, bw=bw),
        grid=(n_seq, nt),
        in_specs=[pl.BlockSpec((tb, width), lambda s, t: (row0 // tb + s * nt + t, col0)),
                  pl.BlockSpec((1, 1, width), lambda s, t: (s, 0, 0)),
                  pl.BlockSpec((1, SUBLANE, width), lambda s, t: (s, 0, 0)),
                  pl.BlockSpec((SUBLANE, width), lambda s, t: (0, 0)),
                  vec(), wspec(), vec(), wspec(), vec(), vec()],
        out_specs=[pl.BlockSpec((tb, width), lambda s, t: (s * nt + t, 0)),
                   pl.BlockSpec((1, 1, width), lambda s, t: (s, 0, 0))],
        out_shape=[jax.ShapeDtypeStruct((n_seq * seq, width), BF16),
                   jax.ShapeDtypeStruct((n_seq, 1, width), F32)],
        scratch_shapes=[pltpu.VMEM((SUBLANE, width), F32), pltpu.VMEM((1, width), F32),
                        pltpu.VMEM((tb, width), F32), pltpu.VMEM((tb, width), F32)],
        compiler_params=_cparams("parallel", "arbitrary"),
        name="rglru",
    )(u_arr, h0.reshape(n_seq, 1, width), conv0p, cwp, conv_b.reshape(1, -1), w_a, b_a.reshape(1, -1),
      w_x, b_x.reshape(1, -1), lam.reshape(1, -1))
    return out, hl.reshape(n_seq, width)


def _permute_rows(perm, x):
    hi = x.astype(BF16)
    rest = x - hi.astype(F32)
    mid = rest.astype(BF16)
    lo = (rest - mid.astype(F32)).astype(BF16)
    dot = lambda piece: jnp.dot(perm, piece, preferred_element_type=F32)
    return (dot(hi) + dot(mid)) + dot(lo)


def _s5_kernel(u_ref, x0_ref, pre_ref, pim_ref, wb_ref, wc_ref, d_ref, z_ref, xl_ref, up_ref, xs_ref, xc_ref,
               *, n_kb, strip):
    t = pl.program_id(1)
    tb, width = u_ref.shape
    seg = tb // SUBLANE
    half = xs_ref.shape[1] // 2
    kw = width // n_kb
    sw = half // n_kb

    @pl.when(t == 0)
    def _():
        xc_ref[...] = x0_ref[0]

    idx0 = lax.broadcasted_iota(jnp.int32, (tb, tb), 0)
    idx1 = lax.broadcasted_iota(jnp.int32, (tb, tb), 1)
    to_segments = jnp.where(idx1 == (idx0 % SUBLANE) * seg + idx0 // SUBLANE, 1.0, 0.0).astype(BF16)
    to_time = jnp.where(idx0 == (idx1 % SUBLANE) * seg + idx1 // SUBLANE, 1.0, 0.0).astype(BF16)
    ub = jnp.dot(to_segments, u_ref[...].astype(BF16), preferred_element_type=F32).astype(BF16)
    for kb in range(n_kb):
        bu = jnp.dot(ub[:, kb * kw:(kb + 1) * kw], wb_ref[kb], preferred_element_type=F32)
        xs_ref[:, kb * sw:(kb + 1) * sw] = bu[:, :sw]
        xs_ref[:, half + kb * sw:half + (kb + 1) * sw] = bu[:, sw:]

    for c in range(half // strip):
        re = slice(c * strip, (c + 1) * strip)
        im = slice(half + c * strip, half + (c + 1) * strip)
        a_re = jnp.broadcast_to(pre_ref[0:1, re], (SUBLANE, strip))
        a_im = jnp.broadcast_to(pim_ref[0:1, re], (SUBLANE, strip))

        def local(j, carry):
            x_re, x_im = carry
            rows = pl.ds(pl.multiple_of(j * SUBLANE, SUBLANE), SUBLANE)
            n_re = a_re * x_re - a_im * x_im + xs_ref[rows, re]
            n_im = a_re * x_im + a_im * x_re + xs_ref[rows, im]
            xs_ref[rows, re] = n_re
            xs_ref[rows, im] = n_im
            return n_re, n_im

        zero = jnp.zeros((SUBLANE, strip), F32)
        e_re, e_im = lax.fori_loop(0, seg, local, (zero, zero), unroll=True)

        al_re, al_im = pre_ref[seg - 1:seg, re], pim_ref[seg - 1:seg, re]
        c_re, c_im = xc_ref[:, re], xc_ref[:, im]
        ins_re, ins_im = [], []
        for s in range(SUBLANE):
            ins_re.append(c_re)
            ins_im.append(c_im)
            c_re, c_im = (al_re * c_re - al_im * c_im + e_re[s:s + 1, :],
                          al_re * c_im + al_im * c_re + e_im[s:s + 1, :])
        xc_ref[:, re] = c_re
        xc_ref[:, im] = c_im
        in_re = jnp.concatenate(ins_re, axis=0)
        in_im = jnp.concatenate(ins_im, axis=0)

        def fix(j, carry):
            rows = pl.ds(pl.multiple_of(j * SUBLANE, SUBLANE), SUBLANE)
            w_re, w_im = pre_ref[pl.ds(j, 1), re], pim_ref[pl.ds(j, 1), re]
            xs_ref[rows, re] = xs_ref[rows, re] + (w_re * in_re - w_im * in_im)
            xs_ref[rows, im] = xs_ref[rows, im] + (w_re * in_im + w_im * in_re)
            return carry

        lax.fori_loop(0, seg, fix, 0, unroll=True)

    xl_ref[0] = xc_ref[...]
    for kb in range(n_kb):
        xb = jnp.concatenate([xs_ref[:, kb * sw:(kb + 1) * sw],
                              xs_ref[:, half + kb * sw:half + (kb + 1) * sw]], axis=1).astype(BF16)
        up_ref[:, kb * kw:(kb + 1) * kw] = jnp.dot(xb, wc_ref[kb], preferred_element_type=F32)
    y = _permute_rows(to_time, up_ref[...])
    z_ref[...] = jax.nn.gelu(y + d_ref[...] * u_ref[...])


def s5_scan(u_arr, row0, col0, n_seq, seq, x0, ld_re, ld_im, wb, wc, d_skip):
    n_kb, kw, two_sw = wb.shape
    width = n_kb * kw
    half = n_kb * two_sw // 2
    tb = _tile(seq, 128)
    nt = seq // tb
    seg = tb // SUBLANE
    assert row0 % tb == 0 and tb % SUBLANE == 0
    steps = jnp.arange(1, seg + 1, dtype=F32)[:, None]
    mag = jnp.exp(steps * ld_re)
    pow_re, pow_im = mag * jnp.cos(steps * ld_im), mag * jnp.sin(steps * ld_im)
    z, xl = pl.pallas_call(
        functools.partial(_s5_kernel, n_kb=n_kb, strip=min(1024, half)),
        grid=(n_seq, nt),
        in_specs=[pl.BlockSpec((tb, width), lambda s, t: (row0 // tb + s * nt + t, col0)),
                  pl.BlockSpec((1, 1, 2 * half), lambda s, t: (s, 0, 0)),
                  pl.BlockSpec((seg, half), lambda s, t: (0, 0)),
                  pl.BlockSpec((seg, half), lambda s, t: (0, 0)),
                  pl.BlockSpec(wb.shape, lambda s, t: (0, 0, 0)),
                  pl.BlockSpec(wc.shape, lambda s, t: (0, 0, 0)),
                  pl.BlockSpec((1, width), lambda s, t: (0, 0))],
        out_specs=[pl.BlockSpec((tb, width), lambda s, t: (s * nt + t, 0)),
                   pl.BlockSpec((1, 1, 2 * half), lambda s, t: (s, 0, 0))],
        out_shape=[jax.ShapeDtypeStruct((n_seq * seq, width), F32),
                   jax.ShapeDtypeStruct((n_seq, 1, 2 * half), F32)],
        scratch_shapes=[pltpu.VMEM((tb, width), F32), pltpu.VMEM((tb, 2 * half), F32),
                        pltpu.VMEM((1, 2 * half), F32)],
        compiler_params=_cparams("parallel", "arbitrary"),
        name="s5_scan",
    )(u_arr, x0, pow_re, pow_im, wb, wc, d_skip.reshape(1, -1))
    return z, xl.reshape(n_seq, 2 * half)


def s5_params(a_re, a_im, log_dt, b_re, b_im, c_re, c_im, kw):
    g, n, gc = b_re.shape
    gpb = kw // gc
    n_kb = g // gpb
    dt = jnp.exp(log_dt)[:, None]
    ld_re, ld_im = a_re * dt, a_im * dt
    e = jnp.exp(ld_re)
    abar_re, abar_im = e * jnp.cos(ld_im), e * jnp.sin(ld_im)
    den = a_re * a_re + a_im * a_im
    q_re = ((abar_re - 1.0) * a_re + abar_im * a_im) / den
    q_im = (abar_im * a_re - (abar_re - 1.0) * a_im) / den
    bb_re = q_re[..., None] * b_re - q_im[..., None] * b_im
    bb_im = q_re[..., None] * b_im + q_im[..., None] * b_re
    eye = jnp.eye(gpb, dtype=F32)

    def pack_b(bb):
        bb = bb.reshape(n_kb, gpb, n, gc)
        return jnp.einsum('kgnc,gh->kgchn', bb, eye).reshape(n_kb, gpb * gc, gpb * n)

    def pack_c(cc):
        cc = cc.reshape(n_kb, gpb, gc, n)
        return jnp.einsum('kgcn,gh->kgnhc', cc, eye).reshape(n_kb, gpb * n, gpb * gc)

    wb = jnp.concatenate([pack_b(bb_re), pack_b(bb_im)], axis=2).astype(BF16)
    wc = jnp.concatenate([pack_c(c_re), pack_c(-c_im)], axis=1).astype(BF16)
    return ld_re.reshape(1, g * n), ld_im.reshape(1, g * n), wb, wc


def _pad_cols(w, n):
    return jnp.pad(w, ((0, 0), (0, n - w.shape[1])))


def _layer(x, h1, cfg, cos, sin, st, p, g_after):
    seq, bsz, sq, past = cfg["seq"], cfg["bsz"], cfg["sq"], cfg["past"]
    t_all, d = x.shape
    n_s = bsz * sq
    width = cfg["mix_w"]
    n_heads, c_heads = cfg["a_heads"], cfg["c_heads"]

    layer = st["layer"]
    ff = p["ffn_w_down"].shape[2]

    def ffn(x, h, idx, g_post, g_next):
        hid = swiglu_up(h, p["ffn_w_up"], (layer, idx))
        y = matmul(hid, p["ffn_w_down"], b_sel=(layer, idx), out_dtype=F32, tm=512, tn=512,
                   tk=ff if ff <= 3 * 4096 else None, name="ffn_down")
        return add_rmsnorm(x, y, g_post, 0.5, g_next)

    g = p["norm_g"]
    x, h2 = ffn(x, h1, 0, g[1], g[2])

    a_cols = cfg["q_lora"] + cfg["kv_lora"] + cfg["a_rope"]
    cpad = _round_up(cfg["q_lora"] + cfg["kv_lora"] + LANE, width)
    c_cols = 3 * c_heads * LANE
    o_b, o_c, o_d, o_g = a_cols, a_cols + width, a_cols + width + c_cols, a_cols + 2 * width + c_cols
    w_in = p["w_in"]
    w_segf = jnp.concatenate([_pad_cols(w_in[:, :a_cols], cpad), w_in[:, o_b:o_c], w_in[:, o_d:o_g]],
                             axis=1).astype(BF16)
    segf = matmul(h2, w_segf, out_dtype=F32, name="w_in_f32")
    qkvc = matmul(h2, w_in[:, o_c:o_d].astype(BF16), out_dtype=F32, name="w_in_qkv")
    gates = matmul(h2, w_in[:, o_g:].astype(BF16), out_dtype=BF16, epilogue=_ep_sigmoid, name="w_in_gates")
    ub_col = cpad // width
    ud_col = ub_col + 1

    nope = LANE
    w_uq = p["mla_w_uq"].reshape(cfg["q_lora"], n_heads, nope + cfg["a_rope"])
    w_uq = jnp.pad(w_uq, ((0, 0), (0, 0), (0, 2 * LANE - nope - cfg["a_rope"])))
    w_uq = w_uq.reshape(cfg["q_lora"], n_heads * 2 * LANE).astype(BF16)
    w_ukv = p["mla_w_ukv"]
    cq, ckv, ckv_b, kr, kr_b = mla_prep(segf, p["mla_q_norm"], p["mla_kv_norm"], cos, sin, cpad)
    tm_q = _tile(t_all, 1024)
    q = matmul(cq, w_uq, out_dtype=BF16, tm=tm_q, tn=1024, name="mla_q",
               epilogue=functools.partial(_ep_rope_q, scale=(nope + cfg["a_rope"]) ** -0.5 * math.log2(math.e),
                                          half=cfg["a_rope"] // 2),
               extra=[(cos, (tm_q, LANE), lambda i, j: (i, 0)), (sin, (tm_q, LANE), lambda i, j: (i, 0))])
    tm_kv = _tile(seq, 1024)
    kv_p = matmul(ckv_b, w_ukv, b_sel=(layer,), out_dtype=BF16, m=seq, tm=tm_kv, widen=2, name="mla_kv_p",
                  epilogue=_ep_pack_kv,
                  extra=[(kr_b, (tm_kv, LANE), lambda i, j: (i, 0))])
    oa_p = mla_flash(q, kv_p, seq, n_heads)
    n_keys = past + sq
    kp = _round_up(n_keys, LANE)
    ckv_all = jnp.concatenate([st["ckv"].astype(BF16), ckv_b[seq:].reshape(bsz, sq, -1),
                               jnp.zeros((bsz, kp - n_keys, cfg["kv_lora"]), BF16)], axis=1)
    kr_all = jnp.concatenate([_pad_cols(st["kr"].reshape(bsz * past, -1), LANE).reshape(bsz, past, LANE).astype(BF16),
                              kr_b[seq:].reshape(bsz, sq, LANE), jnp.zeros((bsz, kp - n_keys, LANE), BF16)], axis=1)
    kv_s = matmul(ckv_all.reshape(bsz * kp, -1), w_ukv, b_sel=(layer,), out_dtype=BF16, name="mla_kv_s")
    oa_s = mla_sample(q, kv_s, kr_all.reshape(bsz * kp, LANE), seq, bsz, sq, n_heads, n_keys, past)
    o_a = jnp.concatenate([oa_p, oa_s], axis=0)

    lru = (p["lru_conv_w"], p["lru_conv_b"], p["lru_w_a"].astype(BF16), p["lru_b_a"],
           p["lru_w_x"].astype(BF16), p["lru_b_x"], p["lru_lambda"])
    n_conv = p["lru_conv_w"].shape[0] - 1
    ob_p, hl_p = rglru(segf, 0, ub_col, 1, seq, jnp.zeros((1, width), F32), jnp.zeros((1, n_conv, width), F32), *lru)
    ob_s, hl_s = rglru(segf, seq, ub_col, bsz, sq, st["h"], st["conv"], *lru)
    o_bb = jnp.concatenate([ob_p, ob_s], axis=0)
    conv_p = segf[seq - n_conv:seq, cpad:cpad + width][None]
    conv_s = segf[seq:, cpad:cpad + width].reshape(bsz, sq, width)[:, sq - n_conv:]

    win = BAND_CHUNKS * CHUNK
    tq_c = win
    assert seq % tq_c == 0
    bias_pp, bias_po = band_bias_tables(p["band_rel_bias"], tq_c, win, tq_c, tq_c)
    hps_p = 2 if c_heads % 2 == 0 else 1
    kcol, vcol = c_heads // hps_p, 2 * c_heads // hps_p
    prev_map = lambda col: (lambda i, g: (jnp.maximum(i - 1, 0), col + g))
    own_map = lambda col: (lambda i, g: (i, col + g))
    oc_p = band_attention(qkvc, 0, 0, qkvc, prev_map(kcol), qkvc, prev_map(vcol), qkvc, own_map(kcol),
                          qkvc, own_map(vcol), bias_pp, bias_po, n_heads=c_heads, hps=hps_p, n_blocks=seq // tq_c,
                          tq=tq_c, prev=win, own=tq_c, mask_first_prev=True)
    own_s = LANE
    k_new = qkvc[seq:, c_heads * LANE:2 * c_heads * LANE].reshape(bsz, sq, c_heads * LANE)
    v_new = qkvc[seq:, 2 * c_heads * LANE:].reshape(bsz, sq, c_heads * LANE)
    pad_own = lambda a: jnp.pad(a, ((0, 0), (0, own_s - sq), (0, 0))).reshape(bsz * own_s, c_heads * LANE)
    bias_sp, bias_so = band_bias_tables(p["band_rel_bias"], sq, win, own_s, sq)
    cache_k, cache_v, cache_blk0 = st["bk_all"], st["bv_all"], st["layer"] * bsz
    batch_map = lambda i, h: (i, h)
    cache_map = lambda i, h: (cache_blk0 + i, h)
    oc_s = band_attention(qkvc, seq, 0, cache_k, cache_map, cache_v, cache_map, pad_own(k_new), batch_map,
                          pad_own(v_new), batch_map, bias_sp, bias_so, n_heads=c_heads, hps=c_heads, n_blocks=bsz,
                          tq=sq, prev=win, own=own_s, mask_first_prev=False)
    o_cc = jnp.concatenate([oc_p, oc_s], axis=0)
    keep = min(win, seq)
    bk_p = qkvc[seq - keep:seq, c_heads * LANE:2 * c_heads * LANE].reshape(1, keep, c_heads, LANE)
    bv_p = qkvc[seq - keep:seq, 2 * c_heads * LANE:].reshape(1, keep, c_heads, LANE)
    bk_s = k_new.reshape(bsz, sq, c_heads, LANE)
    bv_s = v_new.reshape(bsz, sq, c_heads, LANE)

    n_grp, n_state = p["s5_a_re"].shape
    abar_re, abar_im, wb, wc = s5_params(p["s5_a_re"], p["s5_a_im"], p["s5_log_dt"], p["s5_b_re"], p["s5_b_im"],
                                         p["s5_c_re"], p["s5_c_im"], LANE)
    pack0 = lambda re, im: jnp.concatenate([re.reshape(-1, 1, n_grp * n_state), im.reshape(-1, 1, n_grp * n_state)],
                                           axis=2)
    zeros0 = jnp.zeros((1, n_grp, n_state), F32)
    z_p, xl_p = s5_scan(segf, 0, ud_col, 1, seq, pack0(zeros0, zeros0), abar_re, abar_im, wb, wc, p["s5_d"])
    z_s, xl_s = s5_scan(segf, seq, ud_col, bsz, sq, pack0(st["s5re"], st["s5im"]), abar_re, abar_im, wb, wc,
                        p["s5_d"])
    z = jnp.concatenate([z_p, z_s], axis=0)
    tm_g = _tile(t_all, 1024)
    tn_g = _tile(width, 1024)
    o_dd = matmul(z.astype(BF16), p["s5_w_glu"], b_sel=(layer,), out_dtype=BF16, tm=tm_g, tn=tn_g, epilogue=_ep_glu,
                  extra=[(z, (tm_g, tn_g), lambda i, j: (i, j)),
                         (p["s5_b_glu"].reshape(1, -1), (1, tn_g), lambda i, j: (0, j))], name="s5_glu")
    unpack = lambda xl, part: xl[:, part * n_grp * n_state:(part + 1) * n_grp * n_state].reshape(-1, n_grp, n_state)

    merged = merge_branches((o_a, o_bb, o_cc, o_dd), p["w_branch"], layer, gates)
    y = matmul(merged, p["w_out"], b_sel=(layer,), out_dtype=F32, name="w_out")
    x, h3 = add_rmsnorm(x, y, g[3], 1.0, g[4])
    if g_after is None:
        x, h_after = ffn(x, h3, 1, g[5], None), None
    else:
        x, h_after = ffn(x, h3, 1, g[5], g_after)

    new_p = (ckv[:seq][None], kr[:seq, :cfg["a_rope"]][None], hl_p, conv_p, bk_p, bv_p, unpack(xl_p, 0),
             unpack(xl_p, 1))
    new_s = (ckv[seq:].reshape(bsz, sq, -1), kr[seq:, :cfg["a_rope"]].reshape(bsz, sq, -1), hl_s, conv_s, bk_s, bv_s,
             unpack(xl_s, 0), unpack(xl_s, 1))
    return x, h_after, new_p, new_s


def kernel(x_prompt, x_sample, cache_mla_ckv, cache_mla_krope, state_lru_h, state_lru_conv, cache_band_k, cache_band_v, state_s5_re, state_s5_im, norm_g, ffn_w_up, ffn_w_down, w_in, mla_q_norm, mla_kv_norm, mla_w_uq, mla_w_ukv, lru_conv_w, lru_conv_b, lru_w_a, lru_b_a, lru_w_x, lru_b_x, lru_lambda, band_rel_bias, s5_a_re, s5_a_im, s5_log_dt, s5_b_re, s5_b_im, s5_c_re, s5_c_im, s5_d, s5_w_glu, s5_b_glu, w_branch, w_out):
    depth = norm_g.shape[0]
    n_p, seq, d = x_prompt.shape
    bsz, sq, _ = x_sample.shape
    past = cache_mla_ckv.shape[2]
    a_rope = cache_mla_krope.shape[3]
    mix_w = w_branch.shape[2]
    a_heads = mix_w // LANE
    c_heads = cache_band_k.shape[3]
    assert n_p == 1 and a_rope == 2 * ROPE_HALF and cache_band_k.shape[4] == LANE
    assert mla_w_uq.shape[2] == a_heads * (LANE + a_rope) and mla_w_ukv.shape[2] == a_heads * 2 * LANE
    assert past % CHUNK == 0 and cache_band_k.shape[2] == BAND_CHUNKS * CHUNK and past >= BAND_CHUNKS * CHUNK
    assert sq <= CHUNK and seq % (BAND_CHUNKS * CHUNK) == 0 and sq >= lru_conv_w.shape[1] - 1
    cfg = dict(seq=seq, bsz=bsz, sq=sq, past=past, mix_w=mix_w, a_heads=a_heads, c_heads=c_heads,
               q_lora=mla_q_norm.shape[1], kv_lora=mla_kv_norm.shape[1], a_rope=a_rope)

    pos = np.concatenate([np.arange(seq), np.tile(past + np.arange(sq), bsz)]).astype(np.float32)
    half = a_rope // 2
    inv = ROPE_THETA ** (-jnp.arange(half, dtype=F32) / half)
    ang = jnp.asarray(pos)[:, None] * inv
    zpad = jnp.zeros((pos.shape[0], LANE - a_rope), F32)
    cos = jnp.concatenate([jnp.cos(ang), jnp.cos(ang), zpad], axis=1)
    sin = jnp.concatenate([-jnp.sin(ang), jnp.sin(ang), zpad], axis=1)

    x = jnp.concatenate([x_prompt.reshape(seq, d), x_sample.reshape(bsz * sq, d)], axis=0)
    h = rmsnorm(x, norm_g[0, 0])
    stacked = dict(ffn_w_up=ffn_w_up.astype(BF16), ffn_w_down=ffn_w_down.astype(BF16),
                   mla_w_ukv=mla_w_ukv.astype(BF16), s5_w_glu=s5_w_glu.astype(BF16),
                   w_branch=w_branch.astype(BF16), w_out=w_out.astype(BF16))
    new_p, new_s = [], []
    for l in range(depth):
        st = dict(ckv=cache_mla_ckv[l], kr=cache_mla_krope[l], h=state_lru_h[l], conv=state_lru_conv[l],
                  bk_all=cache_band_k.reshape(-1, c_heads * LANE), bv_all=cache_band_v.reshape(-1, c_heads * LANE),
                  layer=l, s5re=state_s5_re[l], s5im=state_s5_im[l])
        p = dict(stacked, norm_g=norm_g[l], w_in=w_in[l],
                 mla_q_norm=mla_q_norm[l], mla_kv_norm=mla_kv_norm[l], mla_w_uq=mla_w_uq[l],
                 lru_conv_w=lru_conv_w[l], lru_conv_b=lru_conv_b[l], lru_w_a=lru_w_a[l], lru_b_a=lru_b_a[l],
                 lru_w_x=lru_w_x[l], lru_b_x=lru_b_x[l], lru_lambda=lru_lambda[l], band_rel_bias=band_rel_bias[l],
                 s5_a_re=s5_a_re[l], s5_a_im=s5_a_im[l], s5_log_dt=s5_log_dt[l], s5_b_re=s5_b_re[l],
                 s5_b_im=s5_b_im[l], s5_c_re=s5_c_re[l], s5_c_im=s5_c_im[l], s5_d=s5_d[l], s5_b_glu=s5_b_glu[l])
        x, h, st_p, st_s = _layer(x, h, cfg, cos, sin, st, p, norm_g[l + 1, 0] if l + 1 < depth else None)
        new_p.append(st_p)
        new_s.append(st_s)

    outs = [x[:seq].reshape(1, seq, d), x[seq:].reshape(bsz, sq, d)]
    for i in range(8):
        outs.append(jnp.stack([s[i] for s in new_p], axis=0))
        outs.append(jnp.stack([s[i] for s in new_s], axis=0))
    return tuple(outs)
```
